```python
import jax
import jax.numpy as jnp
from jax import lax
import numpy as np


D_MODEL = 1024
BATCH = 8
SEQ = 2048
DEPTH = 4

GRID_W = 64
CTX_LEN = 256
CONV_DIM = 512
CONV_WIDTH = 31
MLSTM_HEADS = 4
MLSTM_HEAD_DIM = 128
MLSTM_DIM = MLSTM_HEADS * MLSTM_HEAD_DIM
MIX_DIM = CONV_DIM + MLSTM_DIM
N_GATES = 2 * 2 * MLSTM_HEADS
Q_OFF = 2 * CONV_DIM
K_OFF = Q_OFF + MLSTM_DIM
V_OFF = K_OFF + MLSTM_DIM
O_OFF = V_OFF + MLSTM_DIM
G_OFF = O_OFF + MLSTM_DIM
IN_DIM = G_OFF + N_GATES
K_SCALE = MLSTM_HEAD_DIM ** -0.5
CHUNK = 64
D_FF = 2816
FFN_CONV_WIDTH = 3
N_MOD = 6
EPS = 1e-6

kernel_name = 'hybrid_conformer_mlstm_prefix_block'


def rms_norm(x, g):
    xf = x.astype(jnp.float32)
    y = xf * lax.rsqrt(jnp.mean(xf * xf, axis=-1, keepdims=True) + EPS)
    return (y * g.astype(jnp.float32)).astype(x.dtype)


def layer_norm(x, g, b):
    xf = x.astype(jnp.float32)
    mu = jnp.mean(xf, axis=-1, keepdims=True)
    var = jnp.mean(jnp.square(xf - mu), axis=-1, keepdims=True)
    return ((xf - mu) * lax.rsqrt(var + EPS) * g.astype(jnp.float32) + b.astype(jnp.float32)).astype(x.dtype)


def head_norm(h, g):
    hf = h.astype(jnp.float32)
    mu = jnp.mean(hf, axis=-1, keepdims=True)
    var = jnp.mean(jnp.square(hf - mu), axis=-1, keepdims=True)
    return (hf - mu) * lax.rsqrt(var + EPS) * g.reshape(MLSTM_HEADS, MLSTM_HEAD_DIM).astype(jnp.float32)


def modulate(h, shift, scale):
    return h * (1 + scale) + shift


def dwconv(x, w, b):
    k, ch = w.shape
    y = lax.conv_general_dilated(x, w[:, None, :].astype(x.dtype), window_strides=(1,),
                                 padding=[(k // 2, k // 2)],
                                 dimension_numbers=('NWC', 'WIO', 'NWC'),
                                 feature_group_count=ch)
    return y + b


def seq_conv(x, w, b, grid):
    bsz, length, ch = x.shape
    if grid:
        rows = length // GRID_W
        return dwconv(x.reshape(bsz * rows, GRID_W, ch), w, b).reshape(bsz, length, ch)
    return dwconv(x, w, b)


def _heads(t):
    bsz, length, _ = t.shape
    return t.reshape(bsz, length, MLSTM_HEADS, MLSTM_HEAD_DIM).transpose(0, 2, 1, 3)


def _gates(g, b):
    bsz, length, _ = g.shape
    return (g + b).astype(jnp.float32).reshape(bsz, length, 2, 2, MLSTM_HEADS).transpose(0, 2, 3, 4, 1)


def project(h, w, b_gates):
    a, gl, q, k, v, o, g = jnp.split(h @ w, [CONV_DIM, Q_OFF, K_OFF, V_OFF, O_OFF, G_OFF], axis=-1)
    return a, gl, _heads(q), _heads(k) * K_SCALE, _heads(v), o, _gates(g, b_gates)


def project_state_inputs(h, w, b_gates):
    k, v = jnp.split(h @ w[:, K_OFF:O_OFF], 2, axis=-1)
    g = h @ w[:, G_OFF:]
    return _heads(k) * K_SCALE, _heads(v), _gates(g, b_gates)


def zero_state(bsz):
    return (jnp.zeros((bsz, MLSTM_HEADS, MLSTM_HEAD_DIM, MLSTM_HEAD_DIM), jnp.float32),
            jnp.zeros((bsz, MLSTM_HEADS, MLSTM_HEAD_DIM), jnp.float32),
            jnp.zeros((bsz, MLSTM_HEADS), jnp.float32))


def chunk_gates(itil, logf):
    bsz, nh, length = itil.shape
    nc = length // CHUNK
    return itil.reshape(bsz, nh, nc, CHUNK), jnp.cumsum(logf.reshape(bsz, nh, nc, CHUNK), axis=-1)


def mlstm_states(kc, vc, ic, bcum, state):
    b_last = bcum[..., -1]
    a = b_last[..., None] - bcum + ic
    m_loc = jnp.max(a, axis=-1)
    w = jnp.exp(a - m_loc[..., None])
    c_loc = jnp.einsum('bhcl,bhclv,bhclk->bhcvk', w, vc, kc)
    n_loc = jnp.einsum('bhcl,bhclk->bhck', w, kc)

    def step(carry, inp):
        c_prev, n_prev, m_prev = carry
        c_l, n_l, m_l, b_l = inp
        m_new = jnp.maximum(b_l + m_prev, m_l)
        s_prev = jnp.exp(b_l + m_prev - m_new)
        s_loc = jnp.exp(m_l - m_new)
        c_new = s_prev[..., None, None] * c_prev + s_loc[..., None, None] * c_l
        n_new = s_prev[..., None] * n_prev + s_loc[..., None] * n_l
        return (c_new, n_new, m_new), (c_prev, n_prev, m_prev)

    xs = tuple(jnp.moveaxis(t, 2, 0) for t in (c_loc, n_loc, m_loc, b_last))
    final, entering = lax.scan(step, state, xs)
    entering = tuple(jnp.moveaxis(t, 0, 2) for t in entering)
    return entering, final


def mlstm_direction(q, k, v, itil, logf, state):
    bsz, nh, length, dh = q.shape
    nc = length // CHUNK
    qc, kc, vc = (t.reshape(bsz, nh, nc, CHUNK, dh) for t in (q, k, v))
    ic, bcum = chunk_gates(itil, logf)
    (c_in, n_in, m_in), final = mlstm_states(kc, vc, ic, bcum, state)
    order = jnp.arange(CHUNK)[:, None] >= jnp.arange(CHUNK)[None, :]
    d_log = jnp.where(order, bcum[..., :, None] - bcum[..., None, :] + ic[..., None, :], -jnp.inf)
    g_log = bcum + m_in[..., None]
    m_j = jnp.maximum(g_log, jnp.max(d_log, axis=-1))
    s = jnp.einsum('bhcjd,bhcsd->bhcjs', qc, kc) * jnp.exp(d_log - m_j[..., None])
    w_inter = jnp.exp(g_log - m_j)
    num = (jnp.einsum('bhcjs,bhcsd->bhcjd', s, vc)
           + w_inter[..., None] * jnp.einsum('bhcvk,bhcjk->bhcjv', c_in, qc))
    den = jnp.sum(s, axis=-1) + w_inter * jnp.einsum('bhck,bhcjk->bhcj', n_in, qc)
    h = num / jnp.maximum(jnp.abs(den), jnp.exp(-m_j))[..., None]
    return h.reshape(bsz, nh, length, dh), final


def _flip(t):
    return jnp.flip(t, axis=2)


def mlstm_bidir(q, k, v, gates, init):
    h_f, fin_f = mlstm_direction(q, k, v, gates[:, 0, 0], jax.nn.log_sigmoid(gates[:, 0, 1]), init[0])
    h_b, fin_b = mlstm_direction(_flip(q), _flip(k), _flip(v), _flip(gates[:, 1, 0]),
                                 _flip(jax.nn.log_sigmoid(gates[:, 1, 1])), init[1])
    return h_f + _flip(h_b), (fin_f, fin_b)


def mlstm_final_state(k, v, itil, logf, state):
    bsz, nh, length, dh = k.shape
    nc = length // CHUNK
    kc, vc = (t.reshape(bsz, nh, nc, CHUNK, dh) for t in (k, v))
    ic, bcum = chunk_gates(itil, logf)
    return mlstm_states(kc, vc, ic, bcum, state)[1]


def mlstm_bidir_final(k, v, gates, init):
    fin_f = mlstm_final_state(k, v, gates[:, 0, 0], jax.nn.log_sigmoid(gates[:, 0, 1]), init[0])
    fin_b = mlstm_final_state(_flip(k), _flip(v), _flip(gates[:, 1, 0]),
                              _flip(jax.nn.log_sigmoid(gates[:, 1, 1])), init[1])
    return (fin_f, fin_b)


def mixer_output(a, gl, o, hm, conv_w, conv_b, ln_g, ln_b, head_g, w_out, grid):
    u = a * jax.nn.sigmoid(gl)
    u = jax.nn.silu(layer_norm(seq_conv(u, conv_w, conv_b, grid), ln_g, ln_b))
    bsz, nh, length, dh = hm.shape
    m = jax.nn.sigmoid(o) * head_norm(hm.transpose(0, 2, 1, 3), head_g).reshape(bsz, length, MLSTM_DIM).astype(o.dtype)
    return jnp.concatenate([u, m.astype(u.dtype)], axis=-1) @ w_out


def conv_ffn(h, w_up, cw, cb, w_down, grid):
    val, gate = jnp.split(h @ w_up, 2, axis=-1)
    return (jax.nn.silu(seq_conv(gate, cw, cb, grid)) * val) @ w_down


def setup_inputs(seed: int = 0) -> dict:
    key = jax.random.key(seed)
    ks = jax.random.split(key, 20)

    def nrm(k, shape, std):
        return std * jax.random.normal(k, shape, jnp.float32)

    x = nrm(ks[0], (BATCH, SEQ, D_MODEL), 1.0)
    c = nrm(ks[1], (BATCH, D_MODEL), 1.0)
    ctx = nrm(ks[2], (BATCH, CTX_LEN, D_MODEL), 1.0)
    c_ctx = nrm(ks[3], (D_MODEL,), 1.0)
    w_ada = nrm(ks[4], (DEPTH, D_MODEL, N_MOD * D_MODEL), 0.5 * D_MODEL ** -0.5)
    b_ada = nrm(ks[5], (DEPTH, N_MOD * D_MODEL), 0.02)
    norm_g = 1.0 + nrm(ks[6], (DEPTH, 4, D_MODEL), 0.02)
    w_in = nrm(ks[7], (DEPTH, D_MODEL, IN_DIM), D_MODEL ** -0.5)
    f_bias = jnp.linspace(3.0, 6.0, MLSTM_HEADS, dtype=jnp.float32) + nrm(ks[8], (DEPTH, 2, 1, MLSTM_HEADS), 0.1)
    i_bias = nrm(ks[9], (DEPTH, 2, 1, MLSTM_HEADS), 0.1)
    b_gates = jnp.concatenate([i_bias, f_bias], axis=2).reshape(DEPTH, N_GATES)
    conv_w = nrm(ks[10], (DEPTH, CONV_WIDTH, CONV_DIM), CONV_WIDTH ** -0.5)
    conv_b = nrm(ks[11], (DEPTH, CONV_DIM), 0.02)
    conv_ln_g = 1.0 + nrm(ks[12], (DEPTH, CONV_DIM), 0.02)
    conv_ln_b = nrm(ks[13], (DEPTH, CONV_DIM), 0.02)
    mlstm_norm_g = 1.0 + nrm(ks[14], (DEPTH, MLSTM_DIM), 0.02)
    w_out = nrm(ks[15], (DEPTH, MIX_DIM, D_MODEL), MIX_DIM ** -0.5)
    w_up = nrm(ks[16], (DEPTH, D_MODEL, 2 * D_FF), D_MODEL ** -0.5)
    ffn_conv_w = nrm(ks[17], (DEPTH, FFN_CONV_WIDTH, D_FF), FFN_CONV_WIDTH ** -0.5)
    ffn_conv_b = nrm(ks[18], (DEPTH, D_FF), 0.02)
    w_down = nrm(ks[19], (DEPTH, D_FF, D_MODEL), D_FF ** -0.5)
    return {'x': x, 'c': c, 'ctx': ctx, 'c_ctx': c_ctx, 'w_ada': w_ada, 'b_ada': b_ada,
            'norm_g': norm_g, 'w_in': w_in, 'b_gates': b_gates, 'conv_w': conv_w,
            'conv_b': conv_b, 'conv_ln_g': conv_ln_g, 'conv_ln_b': conv_ln_b,
            'mlstm_norm_g': mlstm_norm_g, 'w_out': w_out, 'w_up': w_up,
            'ffn_conv_w': ffn_conv_w, 'ffn_conv_b': ffn_conv_b, 'w_down': w_down}


def reference(x, c, ctx, c_ctx, w_ada, b_ada, norm_g, w_in, b_gates, conv_w, conv_b,
              conv_ln_g, conv_ln_b, mlstm_norm_g, w_out, w_up, ffn_conv_w, ffn_conv_b, w_down):
    bsz = x.shape[0]
    s_x = jax.nn.silu(c)[:, None, :]
    s_c = jax.nn.silu(c_ctx)
    cs = ctx
    zero = zero_state(bsz)
    for l in range(DEPTH):
        last = l == DEPTH - 1
        mx = jnp.split(s_x @ w_ada[l] + b_ada[l], N_MOD, axis=-1)
        mc = jnp.split(s_c @ w_ada[l] + b_ada[l], N_MOD, axis=-1)
        hx = modulate(rms_norm(x, norm_g[l, 0]), mx[0], mx[1])
        hc = modulate(rms_norm(cs, norm_g[l, 0]), mc[0], mc[1])
        if last:
            kc, vc, gtc = project_state_inputs(hc, w_in[l], b_gates[l])
            fin_c = mlstm_bidir_final(kc, vc, gtc, (zero, zero))
        else:
            ac, glc, qc, kc, vc, oc, gtc = project(hc, w_in[l], b_gates[l])
            hm_c, fin_c = mlstm_bidir(qc, kc, vc, gtc, (zero, zero))
            yc = mixer_output(ac, glc, oc, hm_c, conv_w[l], conv_b[l], conv_ln_g[l], conv_ln_b[l],
                              mlstm_norm_g[l], w_out[l], False)
            cs = cs + mc[2] * rms_norm(yc, norm_g[l, 1])
            hc2 = modulate(rms_norm(cs, norm_g[l, 2]), mc[3], mc[4])
            cs = cs + mc[5] * rms_norm(conv_ffn(hc2, w_up[l], ffn_conv_w[l], ffn_conv_b[l], w_down[l], False),
                                       norm_g[l, 3])
        ax, glx, qx, kx, vx, ox, gtx = project(hx, w_in[l], b_gates[l])
        hm_x, _ = mlstm_bidir(qx, kx, vx, gtx, fin_c)
        yx = mixer_output(ax, glx, ox, hm_x, conv_w[l], conv_b[l], conv_ln_g[l], conv_ln_b[l],
                          mlstm_norm_g[l], w_out[l], True)
        x = x + mx[2] * rms_norm(yx, norm_g[l, 1])
        hx2 = modulate(rms_norm(x, norm_g[l, 2]), mx[3], mx[4])
        x = x + mx[5] * rms_norm(conv_ffn(hx2, w_up[l], ffn_conv_w[l], ffn_conv_b[l], w_down[l], True),
                                 norm_g[l, 3])
    return x
```

```python
import functools

import jax
import jax.numpy as jnp
from jax import lax
from jax.experimental import pallas as pl
from jax.experimental.pallas import tpu as pltpu

D_MODEL = 1024
GRID_W = 64
CONV_DIM = 512
CONV_WIDTH = 31
HEADS = 4
HEAD_DIM = 128
MLSTM_DIM = HEADS * HEAD_DIM
N_UNITS = 2 * HEADS
Q_OFF = 2 * CONV_DIM
K_OFF = Q_OFF + MLSTM_DIM
V_OFF = K_OFF + MLSTM_DIM
O_OFF = V_OFF + MLSTM_DIM
G_OFF = O_OFF + MLSTM_DIM
K_SCALE = HEAD_DIM ** -0.5
CHUNK = 64
D_FF = 2816
FF_BLOCK = 256
N_MOD = 6
EPS = 1e-6
LANES = 128
GATE_PAD = 2 * LANES
VMEM_LIMIT = 56 * 1024 * 1024

F32 = jnp.float32
BF16 = jnp.bfloat16
CP_M, CP_WINTER, CP_ENEGM, CP_WSTATE = 0, 8, 16, 24


def _params(n_grid):
    return pltpu.CompilerParams(dimension_semantics=("arbitrary",) * n_grid,
                                vmem_limit_bytes=VMEM_LIMIT)


def _rms(x, g):
    return x * lax.rsqrt(jnp.mean(x * x, axis=-1, keepdims=True) + EPS) * g


def _sigmoid(x):
    return 1.0 / (1.0 + jnp.exp(-x))


def _silu(x):
    return x * _sigmoid(x)


def _ada_kernel(c_ref, w_ref, b_ref, o_ref):
    s = _silu(c_ref[...]).astype(BF16)
    o_ref[0] = jnp.dot(s, w_ref[0].astype(BF16), preferred_element_type=F32) + b_ref[0]


def _ada(cond, w_ada, b_ada):
    depth, d, n = w_ada.shape
    rows = cond.shape[0]
    tn = 1536
    return pl.pallas_call(
        _ada_kernel,
        grid=(depth, n // tn),
        in_specs=[pl.BlockSpec((rows, d), lambda l, j: (0, 0)),
                  pl.BlockSpec((1, d, tn), lambda l, j: (l, 0, j)),
                  pl.BlockSpec((1, 1, tn), lambda l, j: (l, 0, j))],
        out_specs=pl.BlockSpec((1, rows, tn), lambda l, j: (l, 0, j)),
        out_shape=jax.ShapeDtypeStruct((depth, rows, n), F32),
        compiler_params=_params(2),
        name="ada",
    )(cond, w_ada, b_ada.reshape(depth, 1, n))


def _inproj_kernel(seg, x_ref, sh_ref, sc_ref, g_ref, w1_ref, w2_ref, wg_ref, bg_ref,
                   cw_ref, cb_ref, lg_ref, lb_ref,
                   cu_ref, q_ref, k_ref, v_ref, so_ref, gt_ref, u_s, cv_s):
    tm = x_ref.shape[1]
    x = x_ref[0]
    h = _rms(x, g_ref[...]) * (1.0 + sc_ref[0]) + sh_ref[0]
    hb = h.astype(BF16)
    p1 = jnp.dot(hb, w1_ref[...], preferred_element_type=F32)
    u_s[...] = p1[:, 0:CONV_DIM] * _sigmoid(p1[:, CONV_DIM:Q_OFF])
    q_ref[0] = p1[:, Q_OFF:K_OFF].astype(BF16)
    k_ref[0] = (p1[:, K_OFF:V_OFF] * K_SCALE).astype(BF16)
    p2 = jnp.dot(hb, w2_ref[...], preferred_element_type=F32)
    v_ref[0] = p2[:, 0:MLSTM_DIM].astype(BF16)
    so_ref[0] = _sigmoid(p2[:, MLSTM_DIM:]).astype(BF16)
    gt_ref[0] = jnp.dot(hb, wg_ref[...], preferred_element_type=F32) + bg_ref[...]

    pos = lax.broadcasted_iota(jnp.int32, (seg, LANES), 0)
    half = CONV_WIDTH // 2

    def seg_body(s, carry):
        r0 = pl.multiple_of(s * seg, seg)
        for lb in range(CONV_DIM // LANES):
            ls = slice(lb * LANES, (lb + 1) * LANES)
            useg = u_s[pl.ds(r0, seg), ls]
            acc = jnp.zeros((seg, LANES), F32) + cb_ref[:, ls]
            for j in range(CONV_WIDTH):
                d = j - half
                if d == 0:
                    sh = useg
                else:
                    rolled = pltpu.roll(useg, (-d) % seg, 0)
                    valid = (pos + d >= 0) & (pos + d < seg)
                    sh = jnp.where(valid, rolled, 0.0)
                acc = acc + cw_ref[j:j + 1, ls] * sh
            cv_s[pl.ds(r0, seg), ls] = acc
        return carry

    lax.fori_loop(0, tm // seg, seg_body, 0)
    cv = cv_s[...]
    mu = jnp.mean(cv, axis=-1, keepdims=True)
    var = jnp.mean(jnp.square(cv - mu), axis=-1, keepdims=True)
    y = (cv - mu) * lax.rsqrt(var + EPS) * lg_ref[...] + lb_ref[...]
    cu_ref[0] = _silu(y).astype(BF16)


def _inproj(x, shift, scale, g, w1, w2, wg, bg, cw, cb, lg, lb, seg, tm):
    bsz, t, d = x.shape
    tok = lambda n: pl.BlockSpec((1, tm, n), lambda b, i: (b, i, 0))
    full = lambda a: pl.BlockSpec(a.shape, lambda b, i: (0,) * a.ndim)
    mod = pl.BlockSpec((1, 1, d), lambda b, i: (b, 0, 0))
    act = lambda n, dt: jax.ShapeDtypeStruct((bsz, t, n), dt)
    return pl.pallas_call(
        functools.partial(_inproj_kernel, seg),
        grid=(bsz, t // tm),
        in_specs=[tok(d), mod, mod, full(g), full(w1), full(w2), full(wg), full(bg),
                  full(cw), full(cb), full(lg), full(lb)],
        out_specs=[tok(CONV_DIM), tok(MLSTM_DIM), tok(MLSTM_DIM), tok(MLSTM_DIM), tok(MLSTM_DIM),
                   tok(GATE_PAD)],
        out_shape=[act(CONV_DIM, BF16), act(MLSTM_DIM, BF16), act(MLSTM_DIM, BF16),
                   act(MLSTM_DIM, BF16), act(MLSTM_DIM, BF16), act(GATE_PAD, F32)],
        scratch_shapes=[pltpu.VMEM((tm, CONV_DIM), F32), pltpu.VMEM((tm, CONV_DIM), F32)],
        compiler_params=_params(2),
        name="inproj",
    )(x, shift, scale, g, w1, w2, wg, bg, cw, cb, lg, lb)


def _gate_kernel(gt_ref, m0_ref, rrow_ref, cp_ref, sp_ref, mfin_ref,
                 bcum_s, r_s, cm_s, bl_s, ml_s, mina_s, minb_s, mnewa_s, mnewb_s):
    t = gt_ref.shape[1]
    nc = t // CHUNK
    lane = lax.broadcasted_iota(jnp.int32, (1, LANES), 1)
    fwd = (lane % N_UNITS) < HEADS
    pos = lax.broadcasted_iota(jnp.int32, (CHUNK, LANES), 0)

    def scan(x, op, ident):
        xf, xb = x, x
        k = 1
        while k < CHUNK:
            xf = op(xf, jnp.where(pos >= k, pltpu.roll(xf, k, 0), ident))
            xb = op(xb, jnp.where(pos < CHUNK - k, pltpu.roll(xb, CHUNK - k, 0), ident))
            k *= 2
        return jnp.where(fwd, xf, xb)

    def prep(c, carry):
        r0 = pl.multiple_of(c * CHUNK, CHUNK)
        gi = gt_ref[0, pl.ds(r0, CHUNK), 0:LANES]
        gf = gt_ref[0, pl.ds(r0, CHUNK), LANES:GATE_PAD]
        logf = jnp.minimum(gf, 0.0) - jnp.log1p(jnp.exp(-jnp.abs(gf)))
        bcum = scan(logf, jnp.add, 0.0)
        r = gi - bcum
        bcum_s[pl.ds(r0, CHUNK), :] = bcum
        r_s[pl.ds(r0, CHUNK), :] = r
        cm_s[pl.ds(r0, CHUNK), :] = scan(r, jnp.maximum, -jnp.inf)
        bl = jnp.sum(logf, axis=0, keepdims=True)
        bl_s[pl.ds(c, 1), :] = bl
        ml_s[pl.ds(c, 1), :] = bl + jnp.max(r, axis=0, keepdims=True)
        return carry

    lax.fori_loop(0, nc, prep, 0)

    def mscan(i, m):
        j = nc - 1 - i
        bl = jnp.where(fwd, bl_s[pl.ds(i, 1), :], bl_s[pl.ds(j, 1), :])
        ml = jnp.where(fwd, ml_s[pl.ds(i, 1), :], ml_s[pl.ds(j, 1), :])
        m_new = jnp.maximum(bl + m, ml)
        mina_s[pl.ds(i, 1), :] = m
        minb_s[pl.ds(j, 1), :] = m
        mnewa_s[pl.ds(i, 1), :] = m_new
        mnewb_s[pl.ds(j, 1), :] = m_new
        return m_new

    mfin_ref[0] = lax.fori_loop(0, nc, mscan, m0_ref[0])

    m_in_all = jnp.where(fwd, mina_s[...], minb_s[...])
    m_new_all = jnp.where(fwd, mnewa_s[...], mnewb_s[...])
    sprev = jnp.exp(bl_s[...] + m_in_all - m_new_all)
    for u in range(N_UNITS):
        sp_ref[0, u] = jnp.broadcast_to(sprev[:, u:u + 1], (nc, LANES))

    lane_c = lax.broadcasted_iota(jnp.int32, (CHUNK, LANES), 1)

    def finish(c, carry):
        r0 = pl.multiple_of(c * CHUNK, CHUNK)
        m_in = jnp.where(fwd, mina_s[pl.ds(c, 1), :], minb_s[pl.ds(c, 1), :])
        m_new = jnp.where(fwd, mnewa_s[pl.ds(c, 1), :], mnewb_s[pl.ds(c, 1), :])
        r = r_s[pl.ds(r0, CHUNK), :]
        big_m = jnp.maximum(m_in, cm_s[pl.ds(r0, CHUNK), :])
        w_inter = jnp.exp(m_in - big_m)
        e_negm = jnp.exp(-(bcum_s[pl.ds(r0, CHUNK), :] + big_m))
        w_state = jnp.exp(bl_s[pl.ds(c, 1), :] + r - m_new)
        packed = jnp.where(lane_c < CP_WINTER, big_m,
                           jnp.where(lane_c < CP_ENEGM, pltpu.roll(w_inter, CP_WINTER, 1),
                                     jnp.where(lane_c < CP_WSTATE, pltpu.roll(e_negm, CP_ENEGM, 1),
                                               pltpu.roll(w_state, CP_WSTATE, 1))))
        cp_ref[0, pl.ds(r0, CHUNK), :] = packed
        return carry

    lax.fori_loop(0, nc, finish, 0)

    for blk in range(t // LANES):
        tr = r_s[blk * LANES:(blk + 1) * LANES, :].T
        rrow_ref[0, :, blk * LANES:(blk + 1) * LANES] = tr[0:N_UNITS, :]


def _gates(gt, m0):
    bsz, t, _ = gt.shape
    nc = t // CHUNK
    tl = lambda: pltpu.VMEM((t, LANES), F32)
    cl = lambda: pltpu.VMEM((nc, LANES), F32)
    return pl.pallas_call(
        _gate_kernel,
        grid=(bsz,),
        in_specs=[pl.BlockSpec((1, t, GATE_PAD), lambda b: (b, 0, 0)),
                  pl.BlockSpec((1, 1, LANES), lambda b: (b, 0, 0))],
        out_specs=[pl.BlockSpec((1, N_UNITS, t), lambda b: (b, 0, 0)),
                   pl.BlockSpec((1, t, LANES), lambda b: (b, 0, 0)),
                   pl.BlockSpec((1, N_UNITS, nc, LANES), lambda b: (b, 0, 0, 0)),
                   pl.BlockSpec((1, 1, LANES), lambda b: (b, 0, 0))],
        out_shape=[jax.ShapeDtypeStruct((bsz, N_UNITS, t), F32),
                   jax.ShapeDtypeStruct((bsz, t, LANES), F32),
                   jax.ShapeDtypeStruct((bsz, N_UNITS, nc, LANES), F32),
                   jax.ShapeDtypeStruct((bsz, 1, LANES), F32)],
        scratch_shapes=[tl(), tl(), tl(), cl(), cl(), cl(), cl(), cl(), cl()],
        compiler_params=_params(1),
        name="gates",
    )(gt, m0)


def _mlstm_kernel(q_ref, k_ref, v_ref, rrow_ref, cp_ref, sp_ref, ct0_ref, h_ref, ctf_ref,
                  vaug_s, ctin_f, ctin_b, ct_s):
    t = q_ref.shape[1]
    nc = t // CHUNK
    row = lax.broadcasted_iota(jnp.int32, (CHUNK, CHUNK), 0)
    col = lax.broadcasted_iota(jnp.int32, (CHUNK, CHUNK), 1)
    masks = (row >= col, row <= col)
    tn_dims = (((0,), (0,)), ((), ()))
    nt_dims = (((1,), (1,)), ((), ()))

    for hd in range(HEADS):
        hs = slice(hd * HEAD_DIM, (hd + 1) * HEAD_DIM)
        vaug_s[:, 0:HEAD_DIM] = v_ref[0, :, hs]
        vaug_s[:, HEAD_DIM:] = jnp.ones((t, HEAD_DIM), BF16)

        for direction, ctin in enumerate((ctin_f, ctin_b)):
            u = direction * HEADS + hd
            ct_s[...] = ct0_ref[0, u]

            def state_body(i, carry, direction=direction, ctin=ctin, u=u):
                c = i if direction == 0 else nc - 1 - i
                r0 = pl.multiple_of(c * CHUNK, CHUNK)
                ctin[c] = ct_s[...].astype(BF16)
                w_state = cp_ref[0, pl.ds(r0, CHUNK), CP_WSTATE + u:CP_WSTATE + u + 1]
                kw = (k_ref[0, pl.ds(r0, CHUNK), hs].astype(F32) * w_state).astype(BF16)
                loc = lax.dot_general(kw, vaug_s[pl.ds(r0, CHUNK), :], tn_dims,
                                      preferred_element_type=F32)
                sp = sp_ref[0, u, pl.ds(c, 1), :]
                ct_s[:, 0:HEAD_DIM] = ct_s[:, 0:HEAD_DIM] * sp + loc[:, 0:HEAD_DIM]
                ct_s[:, HEAD_DIM:] = ct_s[:, HEAD_DIM:] * sp + loc[:, HEAD_DIM:]
                return carry

            lax.fori_loop(0, nc, state_body, 0)
            ctf_ref[0, u] = ct_s[...]

        def out_body(c, carry):
            r0 = pl.multiple_of(c * CHUNK, CHUNK)
            qc = q_ref[0, pl.ds(r0, CHUNK), hs]
            kc = k_ref[0, pl.ds(r0, CHUNK), hs]
            va = vaug_s[pl.ds(r0, CHUNK), :]
            s = lax.dot_general(qc, kc, nt_dims, preferred_element_type=F32)
            hsum = jnp.zeros((CHUNK, HEAD_DIM), F32)
            for direction, ctin in enumerate((ctin_f, ctin_b)):
                u = direction * HEADS + hd
                rrow = rrow_ref[0, u, pl.ds(c, 1), :]
                cols = cp_ref[0, pl.ds(r0, CHUNK), :]
                big_m = cols[:, CP_M + u:CP_M + u + 1]
                w_inter = cols[:, CP_WINTER + u:CP_WINTER + u + 1]
                e_negm = cols[:, CP_ENEGM + u:CP_ENEGM + u + 1]
                w = jnp.where(masks[direction], jnp.exp(rrow - big_m), 0.0)
                p = (s * w).astype(BF16)
                tot = (jnp.dot(p, va, preferred_element_type=F32)
                       + w_inter * jnp.dot(qc, ctin[c], preferred_element_type=F32))
                hsum = hsum + tot[:, 0:HEAD_DIM] / jnp.maximum(jnp.abs(tot[:, HEAD_DIM:]), e_negm)
            h_ref[0, pl.ds(r0, CHUNK), hs] = hsum
            return carry

        lax.fori_loop(0, nc, out_body, 0)


def _mlstm(q, k, v, rrow, cp, sp, ct0):
    bsz, t, _ = q.shape
    nc = t // CHUNK
    tok = pl.BlockSpec((1, t, MLSTM_DIM), lambda b: (b, 0, 0))
    st = pl.BlockSpec((1, N_UNITS, HEAD_DIM, 2 * HEAD_DIM), lambda b: (b, 0, 0, 0))
    return pl.pallas_call(
        _mlstm_kernel,
        grid=(bsz,),
        in_specs=[tok, tok, tok,
                  pl.BlockSpec((1, N_UNITS, nc, CHUNK), lambda b: (b, 0, 0, 0)),
                  pl.BlockSpec((1, t, LANES), lambda b: (b, 0, 0)),
                  pl.BlockSpec((1, N_UNITS, nc, LANES), lambda b: (b, 0, 0, 0)),
                  st],
        out_specs=[tok, st],
        out_shape=[jax.ShapeDtypeStruct((bsz, t, MLSTM_DIM), F32),
                   jax.ShapeDtypeStruct((bsz, N_UNITS, HEAD_DIM, 2 * HEAD_DIM), F32)],
        scratch_shapes=[pltpu.VMEM((t, 2 * HEAD_DIM), BF16),
                        pltpu.VMEM((nc, HEAD_DIM, 2 * HEAD_DIM), BF16),
                        pltpu.VMEM((nc, HEAD_DIM, 2 * HEAD_DIM), BF16),
                        pltpu.VMEM((HEAD_DIM, 2 * HEAD_DIM), F32)],
        compiler_params=_params(1),
        name="mlstm",
    )(q, k, v, rrow, cp, sp, ct0)


def _post_kernel(seg, x_ref, cu_ref, hm_ref, so_ref, g1_ref, sh2_ref, sc2_ref, g2_ref,
                 hg_ref, ng_ref, wo_ref, wup_ref, fw_ref, fb_ref, wdn_ref, o_ref, acc_s):
    tm = x_ref.shape[1]
    hm = hm_ref[0]
    so = so_ref[0].astype(F32)
    parts = []
    for hd in range(HEADS):
        hs = slice(hd * HEAD_DIM, (hd + 1) * HEAD_DIM)
        hh = hm[:, hs]
        mu = jnp.mean(hh, axis=-1, keepdims=True)
        var = jnp.mean(jnp.square(hh - mu), axis=-1, keepdims=True)
        parts.append(so[:, hs] * ((hh - mu) * lax.rsqrt(var + EPS) * hg_ref[:, hs]))
    m = jnp.concatenate(parts, axis=-1).astype(BF16)
    y = (jnp.dot(cu_ref[0], wo_ref[0:CONV_DIM, :], preferred_element_type=F32)
         + jnp.dot(m, wo_ref[CONV_DIM:, :], preferred_element_type=F32))
    x1 = x_ref[0] + g1_ref[0] * _rms(y, ng_ref[1:2, :])

    h2 = (_rms(x1, ng_ref[2:3, :]) * (1.0 + sc2_ref[0]) + sh2_ref[0]).astype(BF16)
    pos = lax.broadcasted_iota(jnp.int32, (tm, FF_BLOCK), 0) % seg
    first = pos == 0
    last = pos == seg - 1
    acc_s[...] = jnp.zeros_like(acc_s)
    for j in range(D_FF // FF_BLOCK):
        cs = slice(j * FF_BLOCK, (j + 1) * FF_BLOCK)
        gs = slice(D_FF + j * FF_BLOCK, D_FF + (j + 1) * FF_BLOCK)
        val = jnp.dot(h2, wup_ref[:, cs], preferred_element_type=F32)
        gate = jnp.dot(h2, wup_ref[:, gs], preferred_element_type=F32)
        prev = jnp.where(first, 0.0, pltpu.roll(gate, 1, 0))
        nxt = jnp.where(last, 0.0, pltpu.roll(gate, tm - 1, 0))
        conv = (fw_ref[0:1, cs] * prev + fw_ref[1:2, cs] * gate + fw_ref[2:3, cs] * nxt
                + fb_ref[:, cs])
        act = (_silu(conv) * val).astype(BF16)
        acc_s[...] += jnp.dot(act, wdn_ref[cs, :], preferred_element_type=F32)
    o_ref[0] = x1 + g2_ref[0] * _rms(acc_s[...], ng_ref[3:4, :])


def _post(x, cu, hm, so, gate1, shift2, scale2, gate2, head_g, norm_g, w_out, w_up, fw, fb,
          w_down, seg, tm):
    bsz, t, d = x.shape
    tok = lambda n: pl.BlockSpec((1, tm, n), lambda b, i: (b, i, 0))
    full = lambda a: pl.BlockSpec(a.shape, lambda b, i: (0,) * a.ndim,
                                  pipeline_mode=pl.Buffered(1))
    mod = pl.BlockSpec((1, 1, d), lambda b, i: (b, 0, 0))
    return pl.pallas_call(
        functools.partial(_post_kernel, seg),
        grid=(bsz, t // tm),
        in_specs=[tok(d), tok(CONV_DIM), tok(MLSTM_DIM), tok(MLSTM_DIM), mod, mod, mod, mod,
                  full(head_g), full(norm_g), full(w_out), full(w_up), full(fw), full(fb),
                  full(w_down)],
        out_specs=tok(d),
        out_shape=jax.ShapeDtypeStruct((bsz, t, d), F32),
        scratch_shapes=[pltpu.VMEM((tm, d), F32)],
        compiler_params=_params(2),
        name="post",
    )(x, cu, hm, so, gate1, shift2, scale2, gate2, head_g, norm_g, w_out, w_up, fw, fb, w_down)


def _mixer(xs, mod, l, wts, seg, tm, m0, ct0):
    bsz, t, _ = xs.shape
    cu, q, k, v, so, gt = _inproj(xs, mod[:, 0:1], mod[:, 1:2], wts["norm_g"][l, 0:1],
                                  wts["w1"][l], wts["w2"][l], wts["wg"][l], wts["bg"][l],
                                  wts["conv_w"][l], wts["conv_b"][l], wts["ln_g"][l],
                                  wts["ln_b"][l], seg, tm)
    rrow, cp, sp, mfin = _gates(gt, m0)
    hm, ctf = _mlstm(q, k, v, rrow.reshape(bsz, N_UNITS, t // CHUNK, CHUNK), cp, sp, ct0)
    return cu, hm, so, mfin, ctf


def _ffn(xs, mix, mod, l, wts, seg, tm):
    cu, hm, so = mix
    return _post(xs, cu, hm, so, mod[:, 2:3], mod[:, 3:4], mod[:, 4:5], mod[:, 5:6],
                 wts["head_g"][l], wts["norm_g"][l], wts["w_out"][l], wts["w_up"][l],
                 wts["fw"][l], wts["fb"][l], wts["w_down"][l], seg, tm)


def kernel(x, c, ctx, c_ctx, w_ada, b_ada, norm_g, w_in, b_gates, conv_w, conv_b, conv_ln_g,
           conv_ln_b, mlstm_norm_g, w_out, w_up, ffn_conv_w, ffn_conv_b, w_down):
    bsz, t, d = x.shape
    t_ctx = ctx.shape[1]
    depth = w_ada.shape[0]
    rows = -(-(bsz + 1) // 8) * 8
    cond = jnp.zeros((rows, d), F32).at[:bsz].set(c).at[bsz].set(c_ctx)
    mods = _ada(cond, w_ada, b_ada)

    wg_raw = w_in[:, :, G_OFF:].reshape(depth, d, 2, 2, HEADS)
    bg_raw = b_gates.reshape(depth, 1, 2, 2, HEADS)
    pad = lambda a: jnp.pad(a.reshape(a.shape[0], a.shape[1], N_UNITS),
                            ((0, 0), (0, 0), (0, LANES - N_UNITS)))
    wts = {
        "norm_g": norm_g,
        "w1": w_in[:, :, :V_OFF].astype(BF16),
        "w2": w_in[:, :, V_OFF:G_OFF].astype(BF16),
        "wg": jnp.concatenate([pad(wg_raw[:, :, :, 0]), pad(wg_raw[:, :, :, 1])], -1).astype(BF16),
        "bg": jnp.concatenate([pad(bg_raw[:, :, :, 0]), pad(bg_raw[:, :, :, 1])], -1),
        "conv_w": conv_w, "conv_b": conv_b[:, None], "ln_g": conv_ln_g[:, None],
        "ln_b": conv_ln_b[:, None], "head_g": mlstm_norm_g[:, None],
        "w_out": w_out.astype(BF16), "w_up": w_up.astype(BF16),
        "fw": ffn_conv_w, "fb": ffn_conv_b[:, None], "w_down": w_down.astype(BF16),
    }
    m_zero = jnp.zeros((bsz, 1, LANES), F32)
    ct_zero = jnp.zeros((bsz, N_UNITS, HEAD_DIM, 2 * HEAD_DIM), F32)
    tm_x = min(512, t)
    cs = ctx
    for l in range(depth):
        mod_x = mods[l, :bsz].reshape(bsz, N_MOD, d)
        mod_c = jnp.broadcast_to(mods[l, bsz].reshape(1, N_MOD, d), (bsz, N_MOD, d))
        *mix_c, m_c, ct_c = _mixer(cs, mod_c, l, wts, t_ctx, t_ctx, m_zero, ct_zero)
        if l < depth - 1:
            cs = _ffn(cs, mix_c, mod_c, l, wts, t_ctx, t_ctx)
        *mix_x, _, _ = _mixer(x, mod_x, l, wts, GRID_W, tm_x, m_c, ct_c)
        x = _ffn(x, mix_x, mod_x, l, wts, GRID_W, tm_x)
    return x
```

```python
import functools

import jax
import jax.numpy as jnp
from jax import lax
from jax.experimental import pallas as pl
from jax.experimental.pallas import tpu as pltpu

D_MODEL = 1024
GRID_W = 64
CONV_DIM = 512
CONV_WIDTH = 31
HEADS = 4
HEAD_DIM = 128
MLSTM_DIM = HEADS * HEAD_DIM
N_UNITS = 2 * HEADS
Q_OFF = 2 * CONV_DIM
K_OFF = Q_OFF + MLSTM_DIM
V_OFF = K_OFF + MLSTM_DIM
O_OFF = V_OFF + MLSTM_DIM
G_OFF = O_OFF + MLSTM_DIM
K_SCALE = HEAD_DIM ** -0.5
CHUNK = 64
D_FF = 2816
FF_BLOCK = 256
N_MOD = 6
EPS = 1e-6
LANES = 128
GATE_PAD = 2 * LANES
VMEM_LIMIT = 56 * 1024 * 1024

F32 = jnp.float32
BF16 = jnp.bfloat16
PAIR = 2 * CHUNK
N_ROWS = 16
ST_ROWS = HEAD_DIM + N_ROWS
ROW_M, ROW_WINTER, ROW_ENEGM, ROW_WSTATE = 0, 1, 2, 3
N_ROWQ = 4


def _params(n_grid):
    return pltpu.CompilerParams(dimension_semantics=("arbitrary",) * n_grid,
                                vmem_limit_bytes=VMEM_LIMIT)


def _rms(x, g):
    return x * lax.rsqrt(jnp.mean(x * x, axis=-1, keepdims=True) + EPS) * g


def _sigmoid(x):
    return 1.0 / (1.0 + jnp.exp(-x))


def _silu(x):
    return x * _sigmoid(x)


def _ada_kernel(c_ref, w_ref, b_ref, o_ref):
    s = _silu(c_ref[...]).astype(BF16)
    o_ref[0] = jnp.dot(s, w_ref[0].astype(BF16), preferred_element_type=F32) + b_ref[0]


def _ada(cond, w_ada, b_ada):
    depth, d, n = w_ada.shape
    rows = cond.shape[0]
    tn = 1536
    return pl.pallas_call(
        _ada_kernel,
        grid=(depth, n // tn),
        in_specs=[pl.BlockSpec((rows, d), lambda l, j: (0, 0)),
                  pl.BlockSpec((1, d, tn), lambda l, j: (l, 0, j)),
                  pl.BlockSpec((1, 1, tn), lambda l, j: (l, 0, j))],
        out_specs=pl.BlockSpec((1, rows, tn), lambda l, j: (l, 0, j)),
        out_shape=jax.ShapeDtypeStruct((depth, rows, n), F32),
        compiler_params=_params(2),
        name="ada",
    )(cond, w_ada, b_ada.reshape(depth, 1, n))


def _inproj_kernel(seg, x_ref, sh_ref, sc_ref, g_ref, w1_ref, w2_ref, wg_ref, bg_ref,
                   cw_ref, cb_ref, lg_ref, lb_ref,
                   cu_ref, q_ref, k_ref, v_ref, so_ref, gt_ref, u_s, cv_s):
    tm = x_ref.shape[1]
    x = x_ref[0]
    h = _rms(x, g_ref[...]) * (1.0 + sc_ref[0]) + sh_ref[0]
    hb = h.astype(BF16)
    p1 = jnp.dot(hb, w1_ref[...], preferred_element_type=F32)
    u_s[...] = p1[:, 0:CONV_DIM] * _sigmoid(p1[:, CONV_DIM:Q_OFF])
    q_ref[0] = p1[:, Q_OFF:K_OFF].astype(BF16)
    k_ref[0] = (p1[:, K_OFF:V_OFF] * K_SCALE).astype(BF16)
    p2 = jnp.dot(hb, w2_ref[...], preferred_element_type=F32)
    v_ref[0] = p2[:, 0:MLSTM_DIM].astype(BF16)
    so_ref[0] = _sigmoid(p2[:, MLSTM_DIM:]).astype(BF16)
    gt_ref[0] = jnp.dot(hb, wg_ref[...], preferred_element_type=F32) + bg_ref[...]

    pos = lax.broadcasted_iota(jnp.int32, (seg, LANES), 0)
    half = CONV_WIDTH // 2

    def seg_body(s, carry):
        r0 = pl.multiple_of(s * seg, seg)
        for lb in range(CONV_DIM // LANES):
            ls = slice(lb * LANES, (lb + 1) * LANES)
            useg = u_s[pl.ds(r0, seg), ls]
            acc = jnp.zeros((seg, LANES), F32) + cb_ref[:, ls]
            for j in range(CONV_WIDTH):
                d = j - half
                if d == 0:
                    sh = useg
                else:
                    rolled = pltpu.roll(useg, (-d) % seg, 0)
                    valid = (pos + d >= 0) & (pos + d < seg)
                    sh = jnp.where(valid, rolled, 0.0)
                acc = acc + cw_ref[j:j + 1, ls] * sh
            cv_s[pl.ds(r0, seg), ls] = acc
        return carry

    lax.fori_loop(0, tm // seg, seg_body, 0)
    cv = cv_s[...]
    mu = jnp.mean(cv, axis=-1, keepdims=True)
    var = jnp.mean(jnp.square(cv - mu), axis=-1, keepdims=True)
    y = (cv - mu) * lax.rsqrt(var + EPS) * lg_ref[...] + lb_ref[...]
    cu_ref[0] = _silu(y).astype(BF16)


def _inproj(x, shift, scale, g, w1, w2, wg, bg, cw, cb, lg, lb, seg, tm):
    bsz, t, d = x.shape
    tok = lambda n: pl.BlockSpec((1, tm, n), lambda b, i: (b, i, 0))
    full = lambda a: pl.BlockSpec(a.shape, lambda b, i: (0,) * a.ndim)
    mod = pl.BlockSpec((1, 1, d), lambda b, i: (b, 0, 0))
    act = lambda n, dt: jax.ShapeDtypeStruct((bsz, t, n), dt)
    return pl.pallas_call(
        functools.partial(_inproj_kernel, seg),
        grid=(bsz, t // tm),
        in_specs=[tok(d), mod, mod, full(g), full(w1), full(w2), full(wg), full(bg),
                  full(cw), full(cb), full(lg), full(lb)],
        out_specs=[tok(CONV_DIM), tok(MLSTM_DIM), tok(MLSTM_DIM), tok(MLSTM_DIM), tok(MLSTM_DIM),
                   tok(GATE_PAD)],
        out_shape=[act(CONV_DIM, BF16), act(MLSTM_DIM, BF16), act(MLSTM_DIM, BF16),
                   act(MLSTM_DIM, BF16), act(MLSTM_DIM, BF16), act(GATE_PAD, F32)],
        scratch_shapes=[pltpu.VMEM((tm, CONV_DIM), F32), pltpu.VMEM((tm, CONV_DIM), F32)],
        compiler_params=_params(2),
        name="inproj",
    )(x, shift, scale, g, w1, w2, wg, bg, cw, cb, lg, lb)


def _gate_kernel(gt_ref, m0_ref, rows_ref, rcol_ref, sp_ref, mfin_ref,
                 bcum_s, cm_s, q0_s, q1_s, q2_s, q3_s, bl_s, ml_s, mina_s, minb_s, mnewa_s,
                 mnewb_s):
    t = gt_ref.shape[1]
    nc = t // CHUNK
    lane = lax.broadcasted_iota(jnp.int32, (1, LANES), 1)
    fwd = (lane % N_UNITS) < HEADS
    pos = lax.broadcasted_iota(jnp.int32, (CHUNK, LANES), 0)

    def scan(x, op, ident):
        xf, xb = x, x
        k = 1
        while k < CHUNK:
            xf = op(xf, jnp.where(pos >= k, pltpu.roll(xf, k, 0), ident))
            xb = op(xb, jnp.where(pos < CHUNK - k, pltpu.roll(xb, CHUNK - k, 0), ident))
            k *= 2
        return jnp.where(fwd, xf, xb)

    def prep(c, carry):
        r0 = pl.multiple_of(c * CHUNK, CHUNK)
        gi = gt_ref[0, pl.ds(r0, CHUNK), 0:LANES]
        gf = gt_ref[0, pl.ds(r0, CHUNK), LANES:GATE_PAD]
        logf = jnp.minimum(gf, 0.0) - jnp.log1p(jnp.exp(-jnp.abs(gf)))
        bcum = scan(logf, jnp.add, 0.0)
        r = gi - bcum
        bcum_s[pl.ds(r0, CHUNK), :] = bcum
        rcol_ref[0, pl.ds(r0, CHUNK), :] = r
        cm_s[pl.ds(r0, CHUNK), :] = scan(r, jnp.maximum, -jnp.inf)
        bl = jnp.sum(logf, axis=0, keepdims=True)
        bl_s[pl.ds(c, 1), :] = bl
        ml_s[pl.ds(c, 1), :] = bl + jnp.max(r, axis=0, keepdims=True)
        return carry

    lax.fori_loop(0, nc, prep, 0)

    def mscan(i, m):
        j = nc - 1 - i
        bl = jnp.where(fwd, bl_s[pl.ds(i, 1), :], bl_s[pl.ds(j, 1), :])
        ml = jnp.where(fwd, ml_s[pl.ds(i, 1), :], ml_s[pl.ds(j, 1), :])
        m_new = jnp.maximum(bl + m, ml)
        mina_s[pl.ds(i, 1), :] = m
        minb_s[pl.ds(j, 1), :] = m
        mnewa_s[pl.ds(i, 1), :] = m_new
        mnewb_s[pl.ds(j, 1), :] = m_new
        return m_new

    mfin_ref[0] = lax.fori_loop(0, nc, mscan, m0_ref[0])

    m_in_all = jnp.where(fwd, mina_s[...], minb_s[...])
    m_new_all = jnp.where(fwd, mnewa_s[...], mnewb_s[...])
    sprev = jnp.exp(bl_s[...] + m_in_all - m_new_all)
    for u in range(N_UNITS):
        sp_ref[0, u] = jnp.broadcast_to(sprev[:, u:u + 1], (nc, LANES))

    def finish(c, carry):
        r0 = pl.multiple_of(c * CHUNK, CHUNK)
        m_in = jnp.where(fwd, mina_s[pl.ds(c, 1), :], minb_s[pl.ds(c, 1), :])
        m_new = jnp.where(fwd, mnewa_s[pl.ds(c, 1), :], mnewb_s[pl.ds(c, 1), :])
        big_m = jnp.maximum(m_in, cm_s[pl.ds(r0, CHUNK), :])
        q0_s[pl.ds(r0, CHUNK), :] = big_m
        q1_s[pl.ds(r0, CHUNK), :] = jnp.exp(m_in - big_m)
        q2_s[pl.ds(r0, CHUNK), :] = jnp.exp(-(bcum_s[pl.ds(r0, CHUNK), :] + big_m))
        q3_s[pl.ds(r0, CHUNK), :] = jnp.exp(
            bl_s[pl.ds(c, 1), :] + rcol_ref[0, pl.ds(r0, CHUNK), :] - m_new)
        return carry

    lax.fori_loop(0, nc, finish, 0)

    for blk in range(t // LANES):
        bs = slice(blk * LANES, (blk + 1) * LANES)
        for k, src in enumerate((q0_s, q1_s, q2_s, q3_s)):
            rows_ref[0, k * N_UNITS:(k + 1) * N_UNITS, bs] = src[bs, :].T[0:N_UNITS, :]


def _gates(gt, m0):
    bsz, t, _ = gt.shape
    nc = t // CHUNK
    tl = lambda: pltpu.VMEM((t, LANES), F32)
    cl = lambda: pltpu.VMEM((nc, LANES), F32)
    return pl.pallas_call(
        _gate_kernel,
        grid=(bsz,),
        in_specs=[pl.BlockSpec((1, t, GATE_PAD), lambda b: (b, 0, 0)),
                  pl.BlockSpec((1, 1, LANES), lambda b: (b, 0, 0))],
        out_specs=[pl.BlockSpec((1, N_ROWQ * N_UNITS, t), lambda b: (b, 0, 0)),
                   pl.BlockSpec((1, t, LANES), lambda b: (b, 0, 0)),
                   pl.BlockSpec((1, N_UNITS, nc, LANES), lambda b: (b, 0, 0, 0)),
                   pl.BlockSpec((1, 1, LANES), lambda b: (b, 0, 0))],
        out_shape=[jax.ShapeDtypeStruct((bsz, N_ROWQ * N_UNITS, t), F32),
                   jax.ShapeDtypeStruct((bsz, t, LANES), F32),
                   jax.ShapeDtypeStruct((bsz, N_UNITS, nc, LANES), F32),
                   jax.ShapeDtypeStruct((bsz, 1, LANES), F32)],
        scratch_shapes=[tl() for _ in range(6)] + [cl() for _ in range(6)],
        compiler_params=_params(1),
        name="gates",
    )(gt, m0)


def _mlstm_kernel(q_ref, k_ref, v_ref, rows_ref, rcol_ref, sp_ref, c0_ref, h_ref, cf_ref,
                  vt_s, cin_s, st_s):
    t = q_ref.shape[1]
    n_pairs = t // PAIR
    row = lax.broadcasted_iota(jnp.int32, (PAIR, PAIR), 0)
    col = lax.broadcasted_iota(jnp.int32, (PAIR, PAIR), 1)
    same = (row >= CHUNK) == (col >= CHUNK)
    masks = (same & (col >= row), same & (col <= row))
    lane = lax.broadcasted_iota(jnp.int32, (1, PAIR), 1)
    halves = (lane < CHUNK, lane >= CHUNK)
    qrow = lax.broadcasted_iota(jnp.int32, (PAIR, HEAD_DIM), 0)
    nt_dims = (((1,), (1,)), ((), ()))
    heads = [slice(hd * HEAD_DIM, (hd + 1) * HEAD_DIM) for hd in range(HEADS)]

    def rowq(kind, u, p):
        return rows_ref[0, kind * N_UNITS + u, pl.ds(p, 1), :]

    for hd, hs in enumerate(heads):
        for blk in range(n_pairs):
            vt_s[hd, blk, 0:HEAD_DIM, :] = (
                v_ref[0, blk * PAIR:(blk + 1) * PAIR, hs].astype(F32).T.astype(BF16))
            vt_s[hd, blk, HEAD_DIM:, :] = jnp.ones((N_ROWS, PAIR), BF16)
    st_s[...] = c0_ref[0]

    def state_body(i, carry):
        for direction in range(2):
            p = i if direction == 0 else n_pairs - 1 - i
            r0 = pl.multiple_of(p * PAIR, PAIR)
            for hd, hs in enumerate(heads):
                u = direction * HEADS + hd
                vw = vt_s[hd, p].astype(F32) * rowq(ROW_WSTATE, u, p)
                lhs = jnp.concatenate([jnp.where(halves[0], vw, 0.0).astype(BF16),
                                       jnp.where(halves[1], vw, 0.0).astype(BF16)], axis=0)
                loc = jnp.dot(lhs, k_ref[0, pl.ds(r0, PAIR), hs], preferred_element_type=F32)
                st = st_s[u]
                for j in ((0, 1) if direction == 0 else (1, 0)):
                    c = 2 * p + j
                    cin_s[hd, c, direction * ST_ROWS:(direction + 1) * ST_ROWS, :] = st.astype(BF16)
                    st = st * sp_ref[0, u, pl.ds(c, 1), :] + loc[j * ST_ROWS:(j + 1) * ST_ROWS]
                st_s[u] = st
        return carry

    lax.fori_loop(0, n_pairs, state_body, 0)
    cf_ref[0] = st_s[...]

    def out_body(p, carry):
        r0 = pl.multiple_of(p * PAIR, PAIR)
        rcol = rcol_ref[0, pl.ds(r0, PAIR), :]
        for hd, hs in enumerate(heads):
            qp = q_ref[0, pl.ds(r0, PAIR), hs]
            s_t = lax.dot_general(k_ref[0, pl.ds(r0, PAIR), hs], qp, nt_dims,
                                  preferred_element_type=F32)
            pts = []
            for direction in range(2):
                u = direction * HEADS + hd
                w = jnp.exp(rcol[:, u:u + 1] - rowq(ROW_M, u, p))
                pts.append((s_t * jnp.where(masks[direction], w, 0.0)).astype(BF16))
            intra = jnp.dot(vt_s[hd, p], jnp.concatenate(pts, axis=1),
                            preferred_element_type=F32)
            inter = (lax.dot_general(cin_s[hd, 2 * p], jnp.where(qrow < CHUNK, qp, 0), nt_dims,
                                     preferred_element_type=F32)
                     + lax.dot_general(cin_s[hd, 2 * p + 1], jnp.where(qrow >= CHUNK, qp, 0),
                                       nt_dims, preferred_element_type=F32))
            h_t = jnp.zeros((HEAD_DIM, PAIR), F32)
            for direction in range(2):
                u = direction * HEADS + hd
                tot = (intra[:, direction * PAIR:(direction + 1) * PAIR]
                       + rowq(ROW_WINTER, u, p) * inter[direction * ST_ROWS:(direction + 1) * ST_ROWS])
                den = jnp.maximum(jnp.abs(tot[HEAD_DIM:HEAD_DIM + 1]), rowq(ROW_ENEGM, u, p))
                h_t = h_t + tot[0:HEAD_DIM] * (1.0 / den)
            h_ref[0, pl.ds(r0, PAIR), hs] = h_t.T
        return carry

    lax.fori_loop(0, n_pairs, out_body, 0)


def _mlstm(q, k, v, rows, rcol, sp, c0):
    bsz, t, _ = q.shape
    nc = t // CHUNK
    n_pairs = t // PAIR
    tok = pl.BlockSpec((1, t, MLSTM_DIM), lambda b: (b, 0, 0))
    st = pl.BlockSpec((1, N_UNITS, ST_ROWS, HEAD_DIM), lambda b: (b, 0, 0, 0))
    return pl.pallas_call(
        _mlstm_kernel,
        grid=(bsz,),
        in_specs=[tok, tok, tok,
                  pl.BlockSpec((1, N_ROWQ * N_UNITS, n_pairs, PAIR), lambda b: (b, 0, 0, 0)),
                  pl.BlockSpec((1, t, LANES), lambda b: (b, 0, 0)),
                  pl.BlockSpec((1, N_UNITS, nc, LANES), lambda b: (b, 0, 0, 0)),
                  st],
        out_specs=[tok, st],
        out_shape=[jax.ShapeDtypeStruct((bsz, t, MLSTM_DIM), F32),
                   jax.ShapeDtypeStruct((bsz, N_UNITS, ST_ROWS, HEAD_DIM), F32)],
        scratch_shapes=[pltpu.VMEM((HEADS, n_pairs, ST_ROWS, PAIR), BF16),
                        pltpu.VMEM((HEADS, nc, 2 * ST_ROWS, HEAD_DIM), BF16),
                        pltpu.VMEM((N_UNITS, ST_ROWS, HEAD_DIM), F32)],
        compiler_params=_params(1),
        name="mlstm",
    )(q, k, v, rows, rcol, sp, c0)


def _post_kernel(seg, x_ref, cu_ref, hm_ref, so_ref, g1_ref, sh2_ref, sc2_ref, g2_ref,
                 hg_ref, ng_ref, wo_ref, wup_ref, fw_ref, fb_ref, wdn_ref, o_ref, acc_s):
    tm = x_ref.shape[1]
    hm = hm_ref[0]
    so = so_ref[0].astype(F32)
    parts = []
    for hd in range(HEADS):
        hs = slice(hd * HEAD_DIM, (hd + 1) * HEAD_DIM)
        hh = hm[:, hs]
        mu = jnp.mean(hh, axis=-1, keepdims=True)
        var = jnp.mean(jnp.square(hh - mu), axis=-1, keepdims=True)
        parts.append(so[:, hs] * ((hh - mu) * lax.rsqrt(var + EPS) * hg_ref[:, hs]))
    m = jnp.concatenate(parts, axis=-1).astype(BF16)
    y = (jnp.dot(cu_ref[0], wo_ref[0:CONV_DIM, :], preferred_element_type=F32)
         + jnp.dot(m, wo_ref[CONV_DIM:, :], preferred_element_type=F32))
    x1 = x_ref[0] + g1_ref[0] * _rms(y, ng_ref[1:2, :])

    h2 = (_rms(x1, ng_ref[2:3, :]) * (1.0 + sc2_ref[0]) + sh2_ref[0]).astype(BF16)
    pos = lax.broadcasted_iota(jnp.int32, (tm, FF_BLOCK), 0) % seg
    first = pos == 0
    last = pos == seg - 1
    acc_s[...] = jnp.zeros_like(acc_s)
    for j in range(D_FF // FF_BLOCK):
        cs = slice(j * FF_BLOCK, (j + 1) * FF_BLOCK)
        gs = slice(D_FF + j * FF_BLOCK, D_FF + (j + 1) * FF_BLOCK)
        val = jnp.dot(h2, wup_ref[:, cs], preferred_element_type=F32)
        gate = jnp.dot(h2, wup_ref[:, gs], preferred_element_type=F32)
        prev = jnp.where(first, 0.0, pltpu.roll(gate, 1, 0))
        nxt = jnp.where(last, 0.0, pltpu.roll(gate, tm - 1, 0))
        conv = (fw_ref[0:1, cs] * prev + fw_ref[1:2, cs] * gate + fw_ref[2:3, cs] * nxt
                + fb_ref[:, cs])
        act = (_silu(conv) * val).astype(BF16)
        acc_s[...] += jnp.dot(act, wdn_ref[cs, :], preferred_element_type=F32)
    o_ref[0] = x1 + g2_ref[0] * _rms(acc_s[...], ng_ref[3:4, :])


def _post(x, cu, hm, so, gate1, shift2, scale2, gate2, head_g, norm_g, w_out, w_up, fw, fb,
          w_down, seg, tm):
    bsz, t, d = x.shape
    tok = lambda n: pl.BlockSpec((1, tm, n), lambda b, i: (b, i, 0))
    full = lambda a: pl.BlockSpec(a.shape, lambda b, i: (0,) * a.ndim,
                                  pipeline_mode=pl.Buffered(1))
    mod = pl.BlockSpec((1, 1, d), lambda b, i: (b, 0, 0))
    return pl.pallas_call(
        functools.partial(_post_kernel, seg),
        grid=(bsz, t // tm),
        in_specs=[tok(d), tok(CONV_DIM), tok(MLSTM_DIM), tok(MLSTM_DIM), mod, mod, mod, mod,
                  full(head_g), full(norm_g), full(w_out), full(w_up), full(fw), full(fb),
                  full(w_down)],
        out_specs=tok(d),
        out_shape=jax.ShapeDtypeStruct((bsz, t, d), F32),
        scratch_shapes=[pltpu.VMEM((tm, d), F32)],
        compiler_params=_params(2),
        name="post",
    )(x, cu, hm, so, gate1, shift2, scale2, gate2, head_g, norm_g, w_out, w_up, fw, fb, w_down)


def _mixer(xs, mod, l, wts, seg, tm, m0, ct0):
    bsz, t, _ = xs.shape
    cu, q, k, v, so, gt = _inproj(xs, mod[:, 0:1], mod[:, 1:2], wts["norm_g"][l, 0:1],
                                  wts["w1"][l], wts["w2"][l], wts["wg"][l], wts["bg"][l],
                                  wts["conv_w"][l], wts["conv_b"][l], wts["ln_g"][l],
                                  wts["ln_b"][l], seg, tm)
    rows, rcol, sp, mfin = _gates(gt, m0)
    rows = rows.reshape(bsz, N_ROWQ * N_UNITS, t // PAIR, PAIR)
    hm, ctf = _mlstm(q, k, v, rows, rcol, sp, ct0)
    return cu, hm, so, mfin, ctf


def _ffn(xs, mix, mod, l, wts, seg, tm):
    cu, hm, so = mix
    return _post(xs, cu, hm, so, mod[:, 2:3], mod[:, 3:4], mod[:, 4:5], mod[:, 5:6],
                 wts["head_g"][l], wts["norm_g"][l], wts["w_out"][l], wts["w_up"][l],
                 wts["fw"][l], wts["fb"][l], wts["w_down"][l], seg, tm)


def kernel(x, c, ctx, c_ctx, w_ada, b_ada, norm_g, w_in, b_gates, conv_w, conv_b, conv_ln_g,
           conv_ln_b, mlstm_norm_g, w_out, w_up, ffn_conv_w, ffn_conv_b, w_down):
    bsz, t, d = x.shape
    t_ctx = ctx.shape[1]
    depth = w_ada.shape[0]
    rows = -(-(bsz + 1) // 8) * 8
    cond = jnp.zeros((rows, d), F32).at[:bsz].set(c).at[bsz].set(c_ctx)
    mods = _ada(cond, w_ada, b_ada)

    wg_raw = w_in[:, :, G_OFF:].reshape(depth, d, 2, 2, HEADS)
    bg_raw = b_gates.reshape(depth, 1, 2, 2, HEADS)
    pad = lambda a: jnp.pad(a.reshape(a.shape[0], a.shape[1], N_UNITS),
                            ((0, 0), (0, 0), (0, LANES - N_UNITS)))
    wts = {
        "norm_g": norm_g,
        "w1": w_in[:, :, :V_OFF].astype(BF16),
        "w2": w_in[:, :, V_OFF:G_OFF].astype(BF16),
        "wg": jnp.concatenate([pad(wg_raw[:, :, :, 0]), pad(wg_raw[:, :, :, 1])], -1).astype(BF16),
        "bg": jnp.concatenate([pad(bg_raw[:, :, :, 0]), pad(bg_raw[:, :, :, 1])], -1),
        "conv_w": conv_w, "conv_b": conv_b[:, None], "ln_g": conv_ln_g[:, None],
        "ln_b": conv_ln_b[:, None], "head_g": mlstm_norm_g[:, None],
        "w_out": w_out.astype(BF16), "w_up": w_up.astype(BF16),
        "fw": ffn_conv_w, "fb": ffn_conv_b[:, None], "w_down": w_down.astype(BF16),
    }
    m_zero = jnp.zeros((bsz, 1, LANES), F32)
    ct_zero = jnp.zeros((bsz, N_UNITS, ST_ROWS, HEAD_DIM), F32)
    tm_x = min(512, t)
    cs = ctx
    for l in range(depth):
        mod_x = mods[l, :bsz].reshape(bsz, N_MOD, d)
        mod_c = jnp.broadcast_to(mods[l, bsz].reshape(1, N_MOD, d), (bsz, N_MOD, d))
        *mix_c, m_c, ct_c = _mixer(cs, mod_c, l, wts, t_ctx, t_ctx, m_zero, ct_zero)
        if l < depth - 1:
            cs = _ffn(cs, mix_c, mod_c, l, wts, t_ctx, t_ctx)
        *mix_x, _, _ = _mixer(x, mod_x, l, wts, GRID_W, tm_x, m_c, ct_c)
        x = _ffn(x, mix_x, mod_x, l, wts, GRID_W, tm_x)
    return x
```

```python
import functools

import jax
import jax.numpy as jnp
from jax import lax
from jax.experimental import pallas as pl
from jax.experimental.pallas import tpu as pltpu

D_MODEL = 1024
GRID_W = 64
CONV_DIM = 512
CONV_WIDTH = 31
HEADS = 4
HEAD_DIM = 128
MLSTM_DIM = HEADS * HEAD_DIM
N_UNITS = 2 * HEADS
Q_OFF = 2 * CONV_DIM
K_OFF = Q_OFF + MLSTM_DIM
V_OFF = K_OFF + MLSTM_DIM
O_OFF = V_OFF + MLSTM_DIM
G_OFF = O_OFF + MLSTM_DIM
K_SCALE = HEAD_DIM ** -0.5
CHUNK = 64
D_FF = 2816
FF_BLOCK = 256
SUB = 256
TM_INPROJ = 1024
TM_POST = 512
N_MOD = 6
EPS = 1e-6
LANES = 128
SUBLANES = 8
GATE_PAD = 2 * LANES
VMEM_LIMIT = 56 * 1024 * 1024

F32 = jnp.float32
BF16 = jnp.bfloat16
PAIR = 2 * CHUNK
N_ROWS = 16
ST_ROWS = HEAD_DIM + N_ROWS
ROW_M, ROW_WINTER, ROW_ENEGM, ROW_WSTATE = 0, 1, 2, 3
N_ROWQ = 4


def _params(n_grid):
    return pltpu.CompilerParams(dimension_semantics=("arbitrary",) * n_grid,
                                vmem_limit_bytes=VMEM_LIMIT)


def _rms(x, g):
    return x * lax.rsqrt(jnp.mean(x * x, axis=-1, keepdims=True) + EPS) * g


def _sigmoid(x):
    return 1.0 / (1.0 + jnp.exp(-x))


def _silu(x):
    return x * _sigmoid(x)


def _zero_bits_after(val):
    bits = pltpu.bitcast(val[0:SUBLANES, 0:LANES], jnp.uint32)
    return (bits >> 16) >> 16


def _ordered_after(x, zero_bits):
    if zero_bits is None:
        return x
    tiled = jnp.concatenate([zero_bits] * (x.shape[0] // SUBLANES), axis=0)
    tiled = jnp.concatenate([tiled] * (x.shape[1] // LANES), axis=1)
    return pltpu.bitcast(pltpu.bitcast(x, jnp.uint32) | tiled, x.dtype)


def _interleave(a, b):
    na, nb = len(a), len(b)
    order = sorted([((i + 0.5) / na, 0, i) for i in range(na)]
                   + [((i + 0.5) / nb, 1, i) for i in range(nb)])
    for _, which, i in order:
        (a, b)[which][i]()


def _ada_kernel(c_ref, w_ref, b_ref, o_ref):
    s = _silu(c_ref[...]).astype(BF16)
    o_ref[0] = jnp.dot(s, w_ref[0].astype(BF16), preferred_element_type=F32) + b_ref[0]


def _ada(cond, w_ada, b_ada):
    depth, d, n = w_ada.shape
    rows = cond.shape[0]
    tn = 1536
    return pl.pallas_call(
        _ada_kernel,
        grid=(depth, n // tn),
        in_specs=[pl.BlockSpec((rows, d), lambda l, j: (0, 0)),
                  pl.BlockSpec((1, d, tn), lambda l, j: (l, 0, j)),
                  pl.BlockSpec((1, 1, tn), lambda l, j: (l, 0, j))],
        out_specs=pl.BlockSpec((1, rows, tn), lambda l, j: (l, 0, j)),
        out_shape=jax.ShapeDtypeStruct((depth, rows, n), F32),
        compiler_params=_params(2),
        name="ada",
    )(cond, w_ada, b_ada.reshape(depth, 1, n))


def _inproj_kernel(seg, x_ref, mod_ref, ng_ref, w_ref, bg_ref, cw_ref, cb_ref, lg_ref, lb_ref,
                   cu_ref, q_ref, k_ref, v_ref, so_ref, gt_ref, *scratch):
    tm = x_ref.shape[1]
    sub = min(tm, SUB)
    pos = lax.broadcasted_iota(jnp.int32, (seg, LANES), 0)
    half = CONV_WIDTH // 2
    n_sub = tm // sub
    hb_s = scratch[2 * n_sub]
    token = [None]

    def proj_units(r):
        u_s = scratch[2 * r]
        rs = slice(r * sub, (r + 1) * sub)

        def dot(c0, c1):
            res = jnp.dot(hb_s[rs, :], w_ref[0, :, c0:c1], preferred_element_type=F32)
            token[0] = _zero_bits_after(res)
            return res

        def norm():
            h = (_rms(x_ref[0, rs, :], ng_ref[0, 0:1, :]) * (1.0 + mod_ref[0, 0, 1:2, :])
                 + mod_ref[0, 0, 0:1, :])
            hb_s[rs, :] = h.astype(BF16)

        def glu():
            u_s[...] = dot(0, CONV_DIM) * _sigmoid(dot(CONV_DIM, Q_OFF))

        def q():
            q_ref[0, rs, :] = dot(Q_OFF, K_OFF).astype(BF16)

        def k():
            k_ref[0, rs, :] = (dot(K_OFF, V_OFF) * K_SCALE).astype(BF16)

        def v():
            v_ref[0, rs, :] = dot(V_OFF, O_OFF).astype(BF16)

        def o():
            so_ref[0, rs, :] = _sigmoid(dot(O_OFF, G_OFF)).astype(BF16)

        def gates():
            gt_ref[0, rs, :] = dot(G_OFF, G_OFF + GATE_PAD) + bg_ref[0]

        return [norm, glu, q, k, v, o, gates]

    def conv_units(r):
        u_s, cv_s = scratch[2 * r], scratch[2 * r + 1]
        rs = slice(r * sub, (r + 1) * sub)

        def conv(ss, ls):
            useg = _ordered_after(u_s[ss, ls], token[0])
            acc = jnp.zeros((seg, LANES), F32) + cb_ref[0, :, ls]
            for j in range(CONV_WIDTH):
                d = j - half
                if d == 0:
                    sh = useg
                else:
                    rolled = pltpu.roll(useg, (-d) % seg, 0)
                    valid = (pos + d >= 0) & (pos + d < seg)
                    sh = jnp.where(valid, rolled, 0.0)
                acc = acc + cw_ref[0, j:j + 1, ls] * sh
            cv_s[ss, ls] = acc

        def norm():
            cv = cv_s[...]
            mu = jnp.mean(cv, axis=-1, keepdims=True)
            var = jnp.mean(jnp.square(cv - mu), axis=-1, keepdims=True)
            y = (cv - mu) * lax.rsqrt(var + EPS) * lg_ref[0] + lb_ref[0]
            cu_ref[0, rs, :] = _silu(y).astype(BF16)

        units = [functools.partial(conv, slice(s * seg, (s + 1) * seg),
                                   slice(lb * LANES, (lb + 1) * LANES))
                 for s in range(sub // seg) for lb in range(CONV_DIM // LANES)]
        return units + [norm]

    for r in range(n_sub + 1):
        _interleave(proj_units(r) if r < n_sub else [], conv_units(r - 1) if r > 0 else [])


def _layer_spec(a, l, single_buffer=False):
    kw = {"pipeline_mode": pl.Buffered(1)} if single_buffer else {}
    return pl.BlockSpec((1,) + a.shape[1:], lambda b, i: (l,) + (0,) * (a.ndim - 1), **kw)


def _mod_spec(mods, l, row):
    return pl.BlockSpec((1, 1) + mods.shape[2:],
                        lambda b, i: (l, b if row is None else row, 0, 0))


def _inproj(x, mods, l, row, p, seg, tm):
    bsz, t, d = x.shape
    tok = lambda n: pl.BlockSpec((1, tm, n), lambda b, i: (b, i, 0))
    act = lambda n, dt: jax.ShapeDtypeStruct((bsz, t, n), dt)
    small = [p["bg"], p["conv_w"], p["conv_b"], p["ln_g"], p["ln_b"]]
    return pl.pallas_call(
        functools.partial(_inproj_kernel, seg),
        grid=(bsz, t // tm),
        in_specs=[tok(d), _mod_spec(mods, l, row), _layer_spec(p["norm_g"], l),
                  _layer_spec(p["w_inp"], l, True)] + [_layer_spec(a, l) for a in small],
        out_specs=[tok(CONV_DIM), tok(MLSTM_DIM), tok(MLSTM_DIM), tok(MLSTM_DIM), tok(MLSTM_DIM),
                   tok(GATE_PAD)],
        out_shape=[act(CONV_DIM, BF16), act(MLSTM_DIM, BF16), act(MLSTM_DIM, BF16),
                   act(MLSTM_DIM, BF16), act(MLSTM_DIM, BF16), act(GATE_PAD, F32)],
        scratch_shapes=([pltpu.VMEM((min(tm, SUB), CONV_DIM), F32)] * (2 * (-(-tm // SUB)))
                        + [pltpu.VMEM((tm, d), BF16)]),
        compiler_params=_params(2),
        name="inproj",
    )(x, mods, p["norm_g"], p["w_inp"], *small)


def _gate_kernel(gt_ref, m0_ref, rows_ref, rcol_ref, sp_ref, mfin_ref,
                 bcum_s, cm_s, q0_s, q1_s, q2_s, q3_s, bl_s, ml_s, mina_s, minb_s, mnewa_s,
                 mnewb_s):
    t = gt_ref.shape[1]
    nc = t // CHUNK
    lane = lax.broadcasted_iota(jnp.int32, (1, LANES), 1)
    fwd = (lane % N_UNITS) < HEADS
    pos = lax.broadcasted_iota(jnp.int32, (CHUNK, LANES), 0)

    def scan(x, op, ident):
        xf, xb = x, x
        k = 1
        while k < CHUNK:
            xf = op(xf, jnp.where(pos >= k, pltpu.roll(xf, k, 0), ident))
            xb = op(xb, jnp.where(pos < CHUNK - k, pltpu.roll(xb, CHUNK - k, 0), ident))
            k *= 2
        return jnp.where(fwd, xf, xb)

    def prep(c, carry):
        r0 = pl.multiple_of(c * CHUNK, CHUNK)
        gi = gt_ref[0, pl.ds(r0, CHUNK), 0:LANES]
        gf = gt_ref[0, pl.ds(r0, CHUNK), LANES:GATE_PAD]
        logf = jnp.minimum(gf, 0.0) - jnp.log1p(jnp.exp(-jnp.abs(gf)))
        bcum = scan(logf, jnp.add, 0.0)
        r = gi - bcum
        bcum_s[pl.ds(r0, CHUNK), :] = bcum
        rcol_ref[0, pl.ds(r0, CHUNK), :] = r
        cm_s[pl.ds(r0, CHUNK), :] = scan(r, jnp.maximum, -jnp.inf)
        bl = jnp.sum(logf, axis=0, keepdims=True)
        bl_s[pl.ds(c, 1), :] = bl
        ml_s[pl.ds(c, 1), :] = bl + jnp.max(r, axis=0, keepdims=True)
        return carry

    lax.fori_loop(0, nc, prep, 0)

    def mscan(i, m):
        j = nc - 1 - i
        bl = jnp.where(fwd, bl_s[pl.ds(i, 1), :], bl_s[pl.ds(j, 1), :])
        ml = jnp.where(fwd, ml_s[pl.ds(i, 1), :], ml_s[pl.ds(j, 1), :])
        m_new = jnp.maximum(bl + m, ml)
        mina_s[pl.ds(i, 1), :] = m
        minb_s[pl.ds(j, 1), :] = m
        mnewa_s[pl.ds(i, 1), :] = m_new
        mnewb_s[pl.ds(j, 1), :] = m_new
        return m_new

    mfin_ref[0] = lax.fori_loop(0, nc, mscan, m0_ref[0])

    m_in_all = jnp.where(fwd, mina_s[...], minb_s[...])
    m_new_all = jnp.where(fwd, mnewa_s[...], mnewb_s[...])
    sprev = jnp.exp(bl_s[...] + m_in_all - m_new_all)
    for u in range(N_UNITS):
        sp_ref[0, u] = jnp.broadcast_to(sprev[:, u:u + 1], (nc, LANES))

    def finish(c, carry):
        r0 = pl.multiple_of(c * CHUNK, CHUNK)
        m_in = jnp.where(fwd, mina_s[pl.ds(c, 1), :], minb_s[pl.ds(c, 1), :])
        m_new = jnp.where(fwd, mnewa_s[pl.ds(c, 1), :], mnewb_s[pl.ds(c, 1), :])
        big_m = jnp.maximum(m_in, cm_s[pl.ds(r0, CHUNK), :])
        q0_s[pl.ds(r0, CHUNK), :] = big_m
        q1_s[pl.ds(r0, CHUNK), :] = jnp.exp(m_in - big_m)
        q2_s[pl.ds(r0, CHUNK), :] = jnp.exp(-(bcum_s[pl.ds(r0, CHUNK), :] + big_m))
        q3_s[pl.ds(r0, CHUNK), :] = jnp.exp(
            bl_s[pl.ds(c, 1), :] + rcol_ref[0, pl.ds(r0, CHUNK), :] - m_new)
        return carry

    lax.fori_loop(0, nc, finish, 0)

    for blk in range(t // LANES):
        bs = slice(blk * LANES, (blk + 1) * LANES)
        for k, src in enumerate((q0_s, q1_s, q2_s, q3_s)):
            rows_ref[0, k * N_UNITS:(k + 1) * N_UNITS, bs] = src[bs, :].T[0:N_UNITS, :]


def _gates(gt, m0):
    bsz, t, _ = gt.shape
    nc = t // CHUNK
    tl = lambda: pltpu.VMEM((t, LANES), F32)
    cl = lambda: pltpu.VMEM((nc, LANES), F32)
    return pl.pallas_call(
        _gate_kernel,
        grid=(bsz,),
        in_specs=[pl.BlockSpec((1, t, GATE_PAD), lambda b: (b, 0, 0)),
                  pl.BlockSpec((1, 1, LANES), lambda b: (b, 0, 0))],
        out_specs=[pl.BlockSpec((1, N_ROWQ * N_UNITS, t), lambda b: (b, 0, 0)),
                   pl.BlockSpec((1, t, LANES), lambda b: (b, 0, 0)),
                   pl.BlockSpec((1, N_UNITS, nc, LANES), lambda b: (b, 0, 0, 0)),
                   pl.BlockSpec((1, 1, LANES), lambda b: (b, 0, 0))],
        out_shape=[jax.ShapeDtypeStruct((bsz, N_ROWQ * N_UNITS, t), F32),
                   jax.ShapeDtypeStruct((bsz, t, LANES), F32),
                   jax.ShapeDtypeStruct((bsz, N_UNITS, nc, LANES), F32),
                   jax.ShapeDtypeStruct((bsz, 1, LANES), F32)],
        scratch_shapes=[tl() for _ in range(6)] + [cl() for _ in range(6)],
        compiler_params=_params(1),
        name="gates",
    )(gt, m0)


def _mlstm_kernel(q_ref, k_ref, v_ref, rows_ref, rcol_ref, sp_ref, c0_ref, h_ref, cf_ref,
                  vt_s, cin_s, st_s):
    t = q_ref.shape[1]
    n_pairs = t // PAIR
    row = lax.broadcasted_iota(jnp.int32, (PAIR, PAIR), 0)
    col = lax.broadcasted_iota(jnp.int32, (PAIR, PAIR), 1)
    same = (row >= CHUNK) == (col >= CHUNK)
    masks = (same & (col >= row), same & (col <= row))
    lane = lax.broadcasted_iota(jnp.int32, (1, PAIR), 1)
    halves = (lane < CHUNK, lane >= CHUNK)
    qrow = lax.broadcasted_iota(jnp.int32, (PAIR, HEAD_DIM), 0)
    nt_dims = (((1,), (1,)), ((), ()))
    heads = [slice(hd * HEAD_DIM, (hd + 1) * HEAD_DIM) for hd in range(HEADS)]

    def rowq(kind, u, p):
        return rows_ref[0, kind * N_UNITS + u, pl.ds(p, 1), :]

    for hd, hs in enumerate(heads):
        for blk in range(n_pairs):
            vt_s[hd, blk, 0:HEAD_DIM, :] = (
                v_ref[0, blk * PAIR:(blk + 1) * PAIR, hs].astype(F32).T.astype(BF16))
            vt_s[hd, blk, HEAD_DIM:, :] = jnp.ones((N_ROWS, PAIR), BF16)
    st_s[...] = c0_ref[0]

    def state_body(i, carry):
        for direction in range(2):
            p = i if direction == 0 else n_pairs - 1 - i
            r0 = pl.multiple_of(p * PAIR, PAIR)
            for hd, hs in enumerate(heads):
                u = direction * HEADS + hd
                vw = vt_s[hd, p].astype(F32) * rowq(ROW_WSTATE, u, p)
                lhs = jnp.concatenate([jnp.where(halves[0], vw, 0.0).astype(BF16),
                                       jnp.where(halves[1], vw, 0.0).astype(BF16)], axis=0)
                loc = jnp.dot(lhs, k_ref[0, pl.ds(r0, PAIR), hs], preferred_element_type=F32)
                st = st_s[u]
                for j in ((0, 1) if direction == 0 else (1, 0)):
                    cin_s[hd, p, direction * ST_ROWS:(direction + 1) * ST_ROWS,
                          j * HEAD_DIM:(j + 1) * HEAD_DIM] = st.astype(BF16)
                    st = (st * sp_ref[0, u, pl.ds(2 * p + j, 1), :]
                          + loc[j * ST_ROWS:(j + 1) * ST_ROWS])
                st_s[u] = st
        return carry

    lax.fori_loop(0, n_pairs, state_body, 0)
    cf_ref[0] = st_s[...]

    def out_body(p, carry):
        r0 = pl.multiple_of(p * PAIR, PAIR)
        rcol = rcol_ref[0, pl.ds(r0, PAIR), :]
        for hd, hs in enumerate(heads):
            qp = q_ref[0, pl.ds(r0, PAIR), hs]
            s_t = lax.dot_general(k_ref[0, pl.ds(r0, PAIR), hs], qp, nt_dims,
                                  preferred_element_type=F32)
            pts = []
            for direction in range(2):
                u = direction * HEADS + hd
                w = jnp.exp(rcol[:, u:u + 1] - rowq(ROW_M, u, p))
                pts.append((s_t * jnp.where(masks[direction], w, 0.0)).astype(BF16))
            intra = jnp.dot(vt_s[hd, p], jnp.concatenate(pts, axis=1),
                            preferred_element_type=F32)
            q_blk = jnp.concatenate([jnp.where(qrow < CHUNK, qp, 0),
                                     jnp.where(qrow >= CHUNK, qp, 0)], axis=1)
            inter = lax.dot_general(cin_s[hd, p], q_blk, nt_dims,
                                    preferred_element_type=F32)
            h_t = jnp.zeros((HEAD_DIM, PAIR), F32)
            for direction in range(2):
                u = direction * HEADS + hd
                tot = (intra[:, direction * PAIR:(direction + 1) * PAIR]
                       + rowq(ROW_WINTER, u, p) * inter[direction * ST_ROWS:(direction + 1) * ST_ROWS])
                den = jnp.maximum(jnp.abs(tot[HEAD_DIM:HEAD_DIM + 1]), rowq(ROW_ENEGM, u, p))
                h_t = h_t + tot[0:HEAD_DIM] * (1.0 / den)
            h_ref[0, pl.ds(r0, PAIR), hs] = h_t.T
        return carry

    lax.fori_loop(0, n_pairs, out_body, 0)


def _mlstm(q, k, v, rows, rcol, sp, c0):
    bsz, t, _ = q.shape
    nc = t // CHUNK
    n_pairs = t // PAIR
    tok = pl.BlockSpec((1, t, MLSTM_DIM), lambda b: (b, 0, 0))
    st = pl.BlockSpec((1, N_UNITS, ST_ROWS, HEAD_DIM), lambda b: (b, 0, 0, 0))
    return pl.pallas_call(
        _mlstm_kernel,
        grid=(bsz,),
        in_specs=[tok, tok, tok,
                  pl.BlockSpec((1, N_ROWQ * N_UNITS, n_pairs, PAIR), lambda b: (b, 0, 0, 0)),
                  pl.BlockSpec((1, t, LANES), lambda b: (b, 0, 0)),
                  pl.BlockSpec((1, N_UNITS, nc, LANES), lambda b: (b, 0, 0, 0)),
                  st],
        out_specs=[tok, st],
        out_shape=[jax.ShapeDtypeStruct((bsz, t, MLSTM_DIM), F32),
                   jax.ShapeDtypeStruct((bsz, N_UNITS, ST_ROWS, HEAD_DIM), F32)],
        scratch_shapes=[pltpu.VMEM((HEADS, n_pairs, ST_ROWS, PAIR), BF16),
                        pltpu.VMEM((HEADS, n_pairs, 2 * ST_ROWS, 2 * HEAD_DIM), BF16),
                        pltpu.VMEM((N_UNITS, ST_ROWS, HEAD_DIM), F32)],
        compiler_params=_params(1),
        name="mlstm",
    )(q, k, v, rows, rcol, sp, c0)


def _post_kernel(seg, x_ref, cu_ref, hm_ref, so_ref, mod_ref, hg_ref, ng_ref, wo_ref, wup_ref,
                 fw_ref, fb_ref, wdn_ref, o_ref, *scratch):
    tm = x_ref.shape[1]
    sub = min(tm, SUB)
    pos = lax.broadcasted_iota(jnp.int32, (sub, FF_BLOCK), 0) % seg
    first = pos == 0
    last = pos == seg - 1
    n_sub = tm // sub
    token = [None]

    def mixer_units(r):
        rs = slice(r * sub, (r + 1) * sub)
        x1_s, h2_s = scratch[3 * r + 1], scratch[3 * r + 2]
        y = [None]

        def outproj():
            hm = hm_ref[0, rs, :]
            so = so_ref[0, rs, :].astype(F32)
            parts = []
            for hd in range(HEADS):
                hs = slice(hd * HEAD_DIM, (hd + 1) * HEAD_DIM)
                hh = hm[:, hs]
                mu = jnp.mean(hh, axis=-1, keepdims=True)
                var = jnp.mean(jnp.square(hh - mu), axis=-1, keepdims=True)
                parts.append(so[:, hs] * ((hh - mu) * lax.rsqrt(var + EPS) * hg_ref[0, :, hs]))
            m = jnp.concatenate(parts, axis=-1).astype(BF16)
            y[0] = (jnp.dot(cu_ref[0, rs, :], wo_ref[0, 0:CONV_DIM, :], preferred_element_type=F32)
                    + jnp.dot(m, wo_ref[0, CONV_DIM:, :], preferred_element_type=F32))
            token[0] = _zero_bits_after(y[0])

        def residual():
            x1 = x_ref[0, rs, :] + mod_ref[0, 0, 2:3, :] * _rms(y[0], ng_ref[0, 1:2, :])
            x1_s[...] = x1
            h2_s[...] = (_rms(x1, ng_ref[0, 2:3, :]) * (1.0 + mod_ref[0, 0, 4:5, :])
                         + mod_ref[0, 0, 3:4, :]).astype(BF16)
            token[0] = _zero_bits_after(x1)

        return [outproj, residual]

    def ffn_units(r):
        rs = slice(r * sub, (r + 1) * sub)
        act_s, x1_s, h2_s = scratch[3 * r], scratch[3 * r + 1], scratch[3 * r + 2]

        def block(j):
            cs = slice(j * FF_BLOCK, (j + 1) * FF_BLOCK)
            gs = slice(D_FF + j * FF_BLOCK, D_FF + (j + 1) * FF_BLOCK)
            val = jnp.dot(h2_s[...], wup_ref[0, :, cs], preferred_element_type=F32)
            gate = jnp.dot(h2_s[...], wup_ref[0, :, gs], preferred_element_type=F32)
            gate = _ordered_after(gate, token[0])
            prev = jnp.where(first, 0.0, pltpu.roll(gate, 1, 0))
            nxt = jnp.where(last, 0.0, pltpu.roll(gate, sub - 1, 0))
            conv = (fw_ref[0, 0:1, cs] * prev + fw_ref[0, 1:2, cs] * gate
                    + fw_ref[0, 2:3, cs] * nxt + fb_ref[0, :, cs])
            act_s[:, cs] = (_silu(conv) * val).astype(BF16)

        def down():
            ffn = jnp.dot(act_s[...], wdn_ref[0], preferred_element_type=F32)
            o_ref[0, rs, :] = x1_s[...] + mod_ref[0, 0, 5:6, :] * _rms(ffn, ng_ref[0, 3:4, :])

        return [functools.partial(block, j) for j in range(D_FF // FF_BLOCK)] + [down]

    _interleave(mixer_units(0), [])
    for r in range(n_sub):
        token[0] = None
        _interleave(ffn_units(r), mixer_units(r + 1) if r + 1 < n_sub else [])


def _post(x, cu, hm, so, mods, l, row, p, seg, tm):
    bsz, t, d = x.shape
    tok = lambda n: pl.BlockSpec((1, tm, n), lambda b, i: (b, i, 0))
    big = [p["w_out"], p["w_up"]]
    return pl.pallas_call(
        functools.partial(_post_kernel, seg),
        grid=(bsz, t // tm),
        in_specs=[tok(d), tok(CONV_DIM), tok(MLSTM_DIM), tok(MLSTM_DIM), _mod_spec(mods, l, row),
                  _layer_spec(p["head_g"], l), _layer_spec(p["norm_g"], l)]
                 + [_layer_spec(a, l, True) for a in big]
                 + [_layer_spec(p["fw"], l), _layer_spec(p["fb"], l),
                    _layer_spec(p["w_down"], l, True)],
        out_specs=tok(d),
        out_shape=jax.ShapeDtypeStruct((bsz, t, d), F32),
        scratch_shapes=[pltpu.VMEM((min(tm, SUB), D_FF), BF16),
                        pltpu.VMEM((min(tm, SUB), d), F32),
                        pltpu.VMEM((min(tm, SUB), d), BF16)] * (-(-tm // SUB)),
        compiler_params=_params(2),
        name="post",
    )(x, cu, hm, so, mods, p["head_g"], p["norm_g"], *big, p["fw"], p["fb"], p["w_down"])


def _mixer(xs, mods, l, row, p, seg, tm, m0, ct0):
    bsz, t, _ = xs.shape
    cu, q, k, v, so, gt = _inproj(xs, mods, l, row, p, seg, tm)
    rows, rcol, sp, mfin = _gates(gt, m0)
    rows = rows.reshape(bsz, N_ROWQ * N_UNITS, t // PAIR, PAIR)
    hm, ctf = _mlstm(q, k, v, rows, rcol, sp, ct0)
    return cu, hm, so, mfin, ctf


def kernel(x, c, ctx, c_ctx, w_ada, b_ada, norm_g, w_in, b_gates, conv_w, conv_b, conv_ln_g,
           conv_ln_b, mlstm_norm_g, w_out, w_up, ffn_conv_w, ffn_conv_b, w_down):
    bsz, t, d = x.shape
    t_ctx = ctx.shape[1]
    depth = w_ada.shape[0]
    rows = -(-(bsz + 1) // 8) * 8
    cond = jnp.zeros((rows, d), F32).at[:bsz].set(c).at[bsz].set(c_ctx)
    mods = _ada(cond, w_ada, b_ada).reshape(depth, rows, N_MOD, d)

    wg_raw = w_in[:, :, G_OFF:].reshape(depth, d, 2, 2, HEADS)
    bg_raw = b_gates.reshape(depth, 1, 2, 2, HEADS)
    pad = lambda a: jnp.pad(a.reshape(a.shape[0], a.shape[1], N_UNITS),
                            ((0, 0), (0, 0), (0, LANES - N_UNITS)))
    p = {
        "norm_g": norm_g,
        "w_inp": jnp.concatenate([w_in[:, :, :G_OFF], pad(wg_raw[:, :, :, 0]),
                                  pad(wg_raw[:, :, :, 1])], -1).astype(BF16),
        "bg": jnp.concatenate([pad(bg_raw[:, :, :, 0]), pad(bg_raw[:, :, :, 1])], -1),
        "conv_w": conv_w, "conv_b": conv_b[:, None], "ln_g": conv_ln_g[:, None],
        "ln_b": conv_ln_b[:, None], "head_g": mlstm_norm_g[:, None],
        "w_out": w_out.astype(BF16), "w_up": w_up.astype(BF16),
        "fw": ffn_conv_w, "fb": ffn_conv_b[:, None], "w_down": w_down.astype(BF16),
    }
    m_zero = jnp.zeros((bsz, 1, LANES), F32)
    ct_zero = jnp.zeros((bsz, N_UNITS, ST_ROWS, HEAD_DIM), F32)
    tm_in, tm_post = min(TM_INPROJ, t), min(TM_POST, t)
    cs = ctx
    for l in range(depth):
        cu, hm, so, m_c, ct_c = _mixer(cs, mods, l, bsz, p, t_ctx, t_ctx, m_zero, ct_zero)
        if l < depth - 1:
            cs = _post(cs, cu, hm, so, mods, l, bsz, p, t_ctx, t_ctx)
        cu, hm, so, _, _ = _mixer(x, mods, l, None, p, GRID_W, tm_in, m_c, ct_c)
        x = _post(x, cu, hm, so, mods, l, None, p, GRID_W, tm_post)
    return x
```

```python
import functools

import jax
import jax.numpy as jnp
from jax import lax
from jax.experimental import pallas as pl
from jax.experimental.pallas import tpu as pltpu

D_MODEL = 1024
GRID_W = 64
CONV_DIM = 512
CONV_WIDTH = 31
HEADS = 4
HEAD_DIM = 128
MLSTM_DIM = HEADS * HEAD_DIM
N_UNITS = 2 * HEADS
Q_OFF = 2 * CONV_DIM
K_OFF = Q_OFF + MLSTM_DIM
V_OFF = K_OFF + MLSTM_DIM
O_OFF = V_OFF + MLSTM_DIM
G_OFF = O_OFF + MLSTM_DIM
K_SCALE = HEAD_DIM ** -0.5
CHUNK = 64
D_FF = 2816
FF_BLOCK = 256
SUB = 256
TM_INPROJ = 1024
TM_POST = 512
GATE_UNROLL = 4
N_MOD = 6
EPS = 1e-6
LANES = 128
SUBLANES = 8
GATE_PAD = 2 * LANES
VMEM_LIMIT = 56 * 1024 * 1024

F32 = jnp.float32
BF16 = jnp.bfloat16
PAIR = 2 * CHUNK
N_ROWS = 16
ST_ROWS = HEAD_DIM + N_ROWS
ROW_M, ROW_WINTER, ROW_ENEGM, ROW_WSTATE = 0, 1, 2, 3
N_ROWQ = 4


def _params(n_grid):
    return pltpu.CompilerParams(dimension_semantics=("arbitrary",) * n_grid,
                                vmem_limit_bytes=VMEM_LIMIT)


def _rms(x, g):
    return x * lax.rsqrt(jnp.mean(x * x, axis=-1, keepdims=True) + EPS) * g


def _sigmoid(x):
    return 1.0 / (1.0 + jnp.exp(-x))


def _silu(x):
    return x * _sigmoid(x)


def _zero_bits_after(val):
    bits = pltpu.bitcast(val[0:SUBLANES, 0:LANES], jnp.uint32)
    return (bits >> 16) >> 16


def _ordered_after(x, zero_bits):
    if zero_bits is None:
        return x
    tiled = jnp.concatenate([zero_bits] * (x.shape[0] // SUBLANES), axis=0)
    tiled = jnp.concatenate([tiled] * (x.shape[1] // LANES), axis=1)
    return pltpu.bitcast(pltpu.bitcast(x, jnp.uint32) | tiled, x.dtype)


def _interleave(a, b):
    na, nb = len(a), len(b)
    order = sorted([((i + 0.5) / na, 0, i) for i in range(na)]
                   + [((i + 0.5) / nb, 1, i) for i in range(nb)])
    for _, which, i in order:
        (a, b)[which][i]()


def _ada_kernel(c_ref, w_ref, b_ref, o_ref):
    s = _silu(c_ref[...]).astype(BF16)
    o_ref[0] = jnp.dot(s, w_ref[0].astype(BF16), preferred_element_type=F32) + b_ref[0]


def _ada(cond, w_ada, b_ada):
    depth, d, n = w_ada.shape
    rows = cond.shape[0]
    tn = 1536
    return pl.pallas_call(
        _ada_kernel,
        grid=(depth, n // tn),
        in_specs=[pl.BlockSpec((rows, d), lambda l, j: (0, 0)),
                  pl.BlockSpec((1, d, tn), lambda l, j: (l, 0, j)),
                  pl.BlockSpec((1, 1, tn), lambda l, j: (l, 0, j))],
        out_specs=pl.BlockSpec((1, rows, tn), lambda l, j: (l, 0, j)),
        out_shape=jax.ShapeDtypeStruct((depth, rows, n), F32),
        compiler_params=_params(2),
        name="ada",
    )(cond, w_ada, b_ada.reshape(depth, 1, n))


def _inproj_kernel(seg, x_ref, mod_ref, ng_ref, w_ref, bg_ref, cw_ref, cb_ref, lg_ref, lb_ref,
                   cu_ref, q_ref, k_ref, v_ref, so_ref, gt_ref, *scratch):
    tm = x_ref.shape[1]
    sub = min(tm, SUB)
    pos = lax.broadcasted_iota(jnp.int32, (seg, LANES), 0)
    half = CONV_WIDTH // 2
    n_sub = tm // sub
    hb_s = scratch[2 * n_sub]
    token = [None]

    def proj_units(r):
        u_s = scratch[2 * r]
        rs = slice(r * sub, (r + 1) * sub)

        def dot(c0, c1):
            res = jnp.dot(hb_s[rs, :], w_ref[0, :, c0:c1], preferred_element_type=F32)
            token[0] = _zero_bits_after(res)
            return res

        def norm():
            h = (_rms(x_ref[0, rs, :], ng_ref[0, 0:1, :]) * (1.0 + mod_ref[0, 0, 1:2, :])
                 + mod_ref[0, 0, 0:1, :])
            hb_s[rs, :] = h.astype(BF16)

        def glu():
            u_s[...] = dot(0, CONV_DIM) * _sigmoid(dot(CONV_DIM, Q_OFF))

        def q():
            q_ref[0, rs, :] = dot(Q_OFF, K_OFF).astype(BF16)

        def k():
            k_ref[0, rs, :] = (dot(K_OFF, V_OFF) * K_SCALE).astype(BF16)

        def v():
            v_ref[0, rs, :] = dot(V_OFF, O_OFF).astype(BF16)

        def o():
            so_ref[0, rs, :] = _sigmoid(dot(O_OFF, G_OFF)).astype(BF16)

        def gates():
            gt_ref[0, rs, :] = dot(G_OFF, G_OFF + GATE_PAD) + bg_ref[0]

        return [norm, glu, q, k, v, o, gates]

    def conv_units(r):
        u_s, cv_s = scratch[2 * r], scratch[2 * r + 1]
        rs = slice(r * sub, (r + 1) * sub)

        def conv(ss, ls):
            useg = _ordered_after(u_s[ss, ls], token[0])
            acc = jnp.zeros((seg, LANES), F32) + cb_ref[0, :, ls]
            for j in range(CONV_WIDTH):
                d = j - half
                if d == 0:
                    sh = useg
                else:
                    rolled = pltpu.roll(useg, (-d) % seg, 0)
                    valid = (pos + d >= 0) & (pos + d < seg)
                    sh = jnp.where(valid, rolled, 0.0)
                acc = acc + cw_ref[0, j:j + 1, ls] * sh
            cv_s[ss, ls] = acc

        def norm():
            cv = cv_s[...]
            mu = jnp.mean(cv, axis=-1, keepdims=True)
            var = jnp.mean(jnp.square(cv - mu), axis=-1, keepdims=True)
            y = (cv - mu) * lax.rsqrt(var + EPS) * lg_ref[0] + lb_ref[0]
            cu_ref[0, rs, :] = _silu(y).astype(BF16)

        units = [functools.partial(conv, slice(s * seg, (s + 1) * seg),
                                   slice(lb * LANES, (lb + 1) * LANES))
                 for s in range(sub // seg) for lb in range(CONV_DIM // LANES)]
        return units + [norm]

    for r in range(n_sub + 1):
        _interleave(proj_units(r) if r < n_sub else [], conv_units(r - 1) if r > 0 else [])


def _layer_spec(a, l, single_buffer=False):
    kw = {"pipeline_mode": pl.Buffered(1)} if single_buffer else {}
    return pl.BlockSpec((1,) + a.shape[1:], lambda b, i: (l,) + (0,) * (a.ndim - 1), **kw)


def _mod_spec(mods, l, row):
    return pl.BlockSpec((1, 1) + mods.shape[2:],
                        lambda b, i: (l, b if row is None else row, 0, 0))


def _inproj(x, mods, l, row, p, seg, tm):
    bsz, t, d = x.shape
    tok = lambda n: pl.BlockSpec((1, tm, n), lambda b, i: (b, i, 0))
    act = lambda n, dt: jax.ShapeDtypeStruct((bsz, t, n), dt)
    small = [p["bg"], p["conv_w"], p["conv_b"], p["ln_g"], p["ln_b"]]
    return pl.pallas_call(
        functools.partial(_inproj_kernel, seg),
        grid=(bsz, t // tm),
        in_specs=[tok(d), _mod_spec(mods, l, row), _layer_spec(p["norm_g"], l),
                  _layer_spec(p["w_inp"], l, True)] + [_layer_spec(a, l) for a in small],
        out_specs=[tok(CONV_DIM), tok(MLSTM_DIM), tok(MLSTM_DIM), tok(MLSTM_DIM), tok(MLSTM_DIM),
                   tok(GATE_PAD)],
        out_shape=[act(CONV_DIM, BF16), act(MLSTM_DIM, BF16), act(MLSTM_DIM, BF16),
                   act(MLSTM_DIM, BF16), act(MLSTM_DIM, BF16), act(GATE_PAD, F32)],
        scratch_shapes=([pltpu.VMEM((min(tm, SUB), CONV_DIM), F32)] * (2 * (-(-tm // SUB)))
                        + [pltpu.VMEM((tm, d), BF16)]),
        compiler_params=_params(2),
        name="inproj",
    )(x, mods, p["norm_g"], p["w_inp"], *small)


def _gate_kernel(gt_ref, m0_ref, rows_ref, rcol_ref, sp_ref, mfin_ref,
                 bcum_s, cm_s, q0_s, q1_s, q2_s, q3_s, bl_s, ml_s, mina_s, minb_s, mnewa_s,
                 mnewb_s):
    t = gt_ref.shape[1]
    nc = t // CHUNK
    lane = lax.broadcasted_iota(jnp.int32, (1, LANES), 1)
    fwd = (lane % N_UNITS) < HEADS
    pos = lax.broadcasted_iota(jnp.int32, (CHUNK, LANES), 0)

    def scan(x, op, ident):
        xf, xb = x, x
        k = 1
        while k < CHUNK:
            xf = op(xf, jnp.where(pos >= k, pltpu.roll(xf, k, 0), ident))
            xb = op(xb, jnp.where(pos < CHUNK - k, pltpu.roll(xb, CHUNK - k, 0), ident))
            k *= 2
        return jnp.where(fwd, xf, xb)

    def prep(c, carry):
        r0 = pl.multiple_of(c * CHUNK, CHUNK)
        gi = gt_ref[0, pl.ds(r0, CHUNK), 0:LANES]
        gf = gt_ref[0, pl.ds(r0, CHUNK), LANES:GATE_PAD]
        logf = jnp.minimum(gf, 0.0) - jnp.log1p(jnp.exp(-jnp.abs(gf)))
        bcum = scan(logf, jnp.add, 0.0)
        r = gi - bcum
        bcum_s[pl.ds(r0, CHUNK), :] = bcum
        rcol_ref[0, pl.ds(r0, CHUNK), :] = r
        cm_s[pl.ds(r0, CHUNK), :] = scan(r, jnp.maximum, -jnp.inf)
        bl = jnp.sum(logf, axis=0, keepdims=True)
        bl_s[pl.ds(c, 1), :] = bl
        ml_s[pl.ds(c, 1), :] = bl + jnp.max(r, axis=0, keepdims=True)
        return carry

    lax.fori_loop(0, nc, prep, 0, unroll=GATE_UNROLL)

    def mscan(i, m):
        j = nc - 1 - i
        bl = jnp.where(fwd, bl_s[pl.ds(i, 1), :], bl_s[pl.ds(j, 1), :])
        ml = jnp.where(fwd, ml_s[pl.ds(i, 1), :], ml_s[pl.ds(j, 1), :])
        m_new = jnp.maximum(bl + m, ml)
        mina_s[pl.ds(i, 1), :] = m
        minb_s[pl.ds(j, 1), :] = m
        mnewa_s[pl.ds(i, 1), :] = m_new
        mnewb_s[pl.ds(j, 1), :] = m_new
        return m_new

    mfin_ref[0] = lax.fori_loop(0, nc, mscan, m0_ref[0])

    m_in_all = jnp.where(fwd, mina_s[...], minb_s[...])
    m_new_all = jnp.where(fwd, mnewa_s[...], mnewb_s[...])
    sprev = jnp.exp(bl_s[...] + m_in_all - m_new_all)
    for u in range(N_UNITS):
        sp_ref[0, u] = jnp.broadcast_to(sprev[:, u:u + 1], (nc, LANES))

    def finish(c, carry):
        r0 = pl.multiple_of(c * CHUNK, CHUNK)
        m_in = jnp.where(fwd, mina_s[pl.ds(c, 1), :], minb_s[pl.ds(c, 1), :])
        m_new = jnp.where(fwd, mnewa_s[pl.ds(c, 1), :], mnewb_s[pl.ds(c, 1), :])
        big_m = jnp.maximum(m_in, cm_s[pl.ds(r0, CHUNK), :])
        q0_s[pl.ds(r0, CHUNK), :] = big_m
        q1_s[pl.ds(r0, CHUNK), :] = jnp.exp(m_in - big_m)
        q2_s[pl.ds(r0, CHUNK), :] = jnp.exp(-(bcum_s[pl.ds(r0, CHUNK), :] + big_m))
        q3_s[pl.ds(r0, CHUNK), :] = jnp.exp(
            bl_s[pl.ds(c, 1), :] + rcol_ref[0, pl.ds(r0, CHUNK), :] - m_new)
        return carry

    lax.fori_loop(0, nc, finish, 0, unroll=GATE_UNROLL)

    for blk in range(t // LANES):
        bs = slice(blk * LANES, (blk + 1) * LANES)
        for k, src in enumerate((q0_s, q1_s, q2_s, q3_s)):
            rows_ref[0, k * N_UNITS:(k + 1) * N_UNITS, bs] = src[bs, :].T[0:N_UNITS, :]


def _gates(gt, m0):
    bsz, t, _ = gt.shape
    nc = t // CHUNK
    tl = lambda: pltpu.VMEM((t, LANES), F32)
    cl = lambda: pltpu.VMEM((nc, LANES), F32)
    return pl.pallas_call(
        _gate_kernel,
        grid=(bsz,),
        in_specs=[pl.BlockSpec((1, t, GATE_PAD), lambda b: (b, 0, 0)),
                  pl.BlockSpec((1, 1, LANES), lambda b: (b, 0, 0))],
        out_specs=[pl.BlockSpec((1, N_ROWQ * N_UNITS, t), lambda b: (b, 0, 0)),
                   pl.BlockSpec((1, t, LANES), lambda b: (b, 0, 0)),
                   pl.BlockSpec((1, N_UNITS, nc, LANES), lambda b: (b, 0, 0, 0)),
                   pl.BlockSpec((1, 1, LANES), lambda b: (b, 0, 0))],
        out_shape=[jax.ShapeDtypeStruct((bsz, N_ROWQ * N_UNITS, t), F32),
                   jax.ShapeDtypeStruct((bsz, t, LANES), F32),
                   jax.ShapeDtypeStruct((bsz, N_UNITS, nc, LANES), F32),
                   jax.ShapeDtypeStruct((bsz, 1, LANES), F32)],
        scratch_shapes=[tl() for _ in range(6)] + [cl() for _ in range(6)],
        compiler_params=_params(1),
        name="gates",
    )(gt, m0)


def _mlstm_kernel(q_ref, k_ref, v_ref, rows_ref, rcol_ref, sp_ref, c0_ref, h_ref, cf_ref,
                  vt_s, cin_s, st_s):
    t = q_ref.shape[1]
    n_pairs = t // PAIR
    row = lax.broadcasted_iota(jnp.int32, (PAIR, PAIR), 0)
    col = lax.broadcasted_iota(jnp.int32, (PAIR, PAIR), 1)
    same = (row >= CHUNK) == (col >= CHUNK)
    masks = (same & (col >= row), same & (col <= row))
    lane = lax.broadcasted_iota(jnp.int32, (1, PAIR), 1)
    halves = (lane < CHUNK, lane >= CHUNK)
    qrow = lax.broadcasted_iota(jnp.int32, (PAIR, HEAD_DIM), 0)
    nt_dims = (((1,), (1,)), ((), ()))
    heads = [slice(hd * HEAD_DIM, (hd + 1) * HEAD_DIM) for hd in range(HEADS)]

    def rowq(kind, u, p):
        return rows_ref[0, kind * N_UNITS + u, pl.ds(p, 1), :]

    for hd, hs in enumerate(heads):
        for blk in range(n_pairs):
            vt_s[hd, blk, 0:HEAD_DIM, :] = (
                v_ref[0, blk * PAIR:(blk + 1) * PAIR, hs].astype(F32).T.astype(BF16))
            vt_s[hd, blk, HEAD_DIM:, :] = jnp.ones((N_ROWS, PAIR), BF16)
    st_s[...] = c0_ref[0]

    def state_body(i, carry):
        for direction in range(2):
            p = i if direction == 0 else n_pairs - 1 - i
            r0 = pl.multiple_of(p * PAIR, PAIR)
            for hd, hs in enumerate(heads):
                u = direction * HEADS + hd
                vw = vt_s[hd, p].astype(F32) * rowq(ROW_WSTATE, u, p)
                lhs = jnp.concatenate([jnp.where(halves[0], vw, 0.0).astype(BF16),
                                       jnp.where(halves[1], vw, 0.0).astype(BF16)], axis=0)
                loc = jnp.dot(lhs, k_ref[0, pl.ds(r0, PAIR), hs], preferred_element_type=F32)
                st = st_s[u]
                for j in ((0, 1) if direction == 0 else (1, 0)):
                    cin_s[hd, p, direction * ST_ROWS:(direction + 1) * ST_ROWS,
                          j * HEAD_DIM:(j + 1) * HEAD_DIM] = st.astype(BF16)
                    st = (st * sp_ref[0, u, pl.ds(2 * p + j, 1), :]
                          + loc[j * ST_ROWS:(j + 1) * ST_ROWS])
                st_s[u] = st
        return carry

    lax.fori_loop(0, n_pairs, state_body, 0, unroll=2)
    cf_ref[0] = st_s[...]

    def out_body(p, carry):
        r0 = pl.multiple_of(p * PAIR, PAIR)
        rcol = rcol_ref[0, pl.ds(r0, PAIR), :]
        val = [dict() for _ in heads]

        def row_after(x, zero_bits):
            if zero_bits is None:
                return x
            return pltpu.bitcast(pltpu.bitcast(x, jnp.uint32) | zero_bits[0:1], x.dtype)

        def scores(hd):
            hs = heads[hd]
            qp = q_ref[0, pl.ds(r0, PAIR), hs]
            val[hd]["s_t"] = lax.dot_general(k_ref[0, pl.ds(r0, PAIR), hs], qp, nt_dims,
                                             preferred_element_type=F32)
            q_blk = jnp.concatenate([jnp.where(qrow < CHUNK, qp, 0),
                                     jnp.where(qrow >= CHUNK, qp, 0)], axis=1)
            inter = lax.dot_general(cin_s[hd, p], q_blk, nt_dims,
                                    preferred_element_type=F32)
            val[hd]["inter"] = inter
            val[hd]["scores_done"] = _zero_bits_after(inter)

        def weights(hd):
            nxt = val[hd + 1]["scores_done"] if hd + 1 < HEADS else None
            pts = []
            for direction in range(2):
                u = direction * HEADS + hd
                w = jnp.exp(rcol[:, u:u + 1] - row_after(rowq(ROW_M, u, p), nxt))
                pts.append((val[hd]["s_t"] * jnp.where(masks[direction], w, 0.0)).astype(BF16))
            val[hd]["pt"] = jnp.concatenate(pts, axis=1)

        def intra(hd):
            res = jnp.dot(vt_s[hd, p], val[hd]["pt"], preferred_element_type=F32)
            val[hd]["intra"] = res
            val[hd]["intra_done"] = _zero_bits_after(res)

        def combine(hd):
            nxt = val[hd + 1]["intra_done"] if hd + 1 < HEADS else None
            h_t = jnp.zeros((HEAD_DIM, PAIR), F32)
            for direction in range(2):
                u = direction * HEADS + hd
                tot = (val[hd]["intra"][:, direction * PAIR:(direction + 1) * PAIR]
                       + row_after(rowq(ROW_WINTER, u, p), nxt)
                       * val[hd]["inter"][direction * ST_ROWS:(direction + 1) * ST_ROWS])
                den = jnp.maximum(jnp.abs(tot[HEAD_DIM:HEAD_DIM + 1]), rowq(ROW_ENEGM, u, p))
                h_t = h_t + tot[0:HEAD_DIM] * (1.0 / den)
            h_ref[0, pl.ds(r0, PAIR), heads[hd]] = h_t.T

        for stage, hd in ((scores, 0), (scores, 1), (weights, 0), (scores, 2), (intra, 0),
                          (weights, 1), (scores, 3), (intra, 1), (combine, 0), (weights, 2),
                          (intra, 2), (combine, 1), (weights, 3), (intra, 3), (combine, 2),
                          (combine, 3)):
            stage(hd)
        return carry

    lax.fori_loop(0, n_pairs, out_body, 0)


def _mlstm(q, k, v, rows, rcol, sp, c0):
    bsz, t, _ = q.shape
    nc = t // CHUNK
    n_pairs = t // PAIR
    tok = pl.BlockSpec((1, t, MLSTM_DIM), lambda b: (b, 0, 0))
    st = pl.BlockSpec((1, N_UNITS, ST_ROWS, HEAD_DIM), lambda b: (b, 0, 0, 0))
    return pl.pallas_call(
        _mlstm_kernel,
        grid=(bsz,),
        in_specs=[tok, tok, tok,
                  pl.BlockSpec((1, N_ROWQ * N_UNITS, n_pairs, PAIR), lambda b: (b, 0, 0, 0)),
                  pl.BlockSpec((1, t, LANES), lambda b: (b, 0, 0)),
                  pl.BlockSpec((1, N_UNITS, nc, LANES), lambda b: (b, 0, 0, 0)),
                  st],
        out_specs=[tok, st],
        out_shape=[jax.ShapeDtypeStruct((bsz, t, MLSTM_DIM), F32),
                   jax.ShapeDtypeStruct((bsz, N_UNITS, ST_ROWS, HEAD_DIM), F32)],
        scratch_shapes=[pltpu.VMEM((HEADS, n_pairs, ST_ROWS, PAIR), BF16),
                        pltpu.VMEM((HEADS, n_pairs, 2 * ST_ROWS, 2 * HEAD_DIM), BF16),
                        pltpu.VMEM((N_UNITS, ST_ROWS, HEAD_DIM), F32)],
        compiler_params=_params(1),
        name="mlstm",
    )(q, k, v, rows, rcol, sp, c0)


def _post_kernel(seg, carry, *refs):
    cur, refs = refs[:5], refs[5:]
    if carry:
        nxt, refs = refs[:5], refs[5:]
    hg_ref, ng_ref, wo_ref, wup_ref, fw_ref, fb_ref, wdn_ref, o_ref = refs[:8]
    scratch = refs[8:]
    tm = cur[0].shape[1]
    sub = min(tm, SUB)
    pos = lax.broadcasted_iota(jnp.int32, (sub, FF_BLOCK), 0) % seg
    first = pos == 0
    last = pos == seg - 1
    n_sub = tm // sub
    token = [None]

    def mixer_units(src, r):
        x_ref, cu_ref, hm_ref, so_ref, mod_ref = src
        rs = slice(r * sub, (r + 1) * sub)
        x1_s, h2_s = scratch[3 * (r % 2) + 1], scratch[3 * (r % 2) + 2]
        y = [None]

        def outproj():
            hm = hm_ref[0, rs, :]
            so = so_ref[0, rs, :].astype(F32)
            parts = []
            for hd in range(HEADS):
                hs = slice(hd * HEAD_DIM, (hd + 1) * HEAD_DIM)
                hh = hm[:, hs]
                mu = jnp.mean(hh, axis=-1, keepdims=True)
                var = jnp.mean(jnp.square(hh - mu), axis=-1, keepdims=True)
                parts.append(so[:, hs] * ((hh - mu) * lax.rsqrt(var + EPS) * hg_ref[0, :, hs]))
            m = jnp.concatenate(parts, axis=-1).astype(BF16)
            y[0] = (jnp.dot(cu_ref[0, rs, :], wo_ref[0, 0:CONV_DIM, :], preferred_element_type=F32)
                    + jnp.dot(m, wo_ref[0, CONV_DIM:, :], preferred_element_type=F32))
            token[0] = _zero_bits_after(y[0])

        def residual():
            x1 = x_ref[0, rs, :] + mod_ref[0, 0, 2:3, :] * _rms(y[0], ng_ref[0, 1:2, :])
            x1_s[...] = x1
            h2_s[...] = (_rms(x1, ng_ref[0, 2:3, :]) * (1.0 + mod_ref[0, 0, 4:5, :])
                         + mod_ref[0, 0, 3:4, :]).astype(BF16)
            token[0] = _zero_bits_after(x1)

        return [outproj, residual]

    def ffn_units(r):
        rs = slice(r * sub, (r + 1) * sub)
        act_s, x1_s, h2_s = (scratch[3 * (r % 2) + k] for k in range(3))

        def block(j):
            cs = slice(j * FF_BLOCK, (j + 1) * FF_BLOCK)
            gs = slice(D_FF + j * FF_BLOCK, D_FF + (j + 1) * FF_BLOCK)
            val = jnp.dot(h2_s[...], wup_ref[0, :, cs], preferred_element_type=F32)
            gate = jnp.dot(h2_s[...], wup_ref[0, :, gs], preferred_element_type=F32)
            gate = _ordered_after(gate, token[0])
            before = jnp.where(first, 0.0, pltpu.roll(gate, 1, 0))
            after = jnp.where(last, 0.0, pltpu.roll(gate, sub - 1, 0))
            conv = (fw_ref[0, 0:1, cs] * before + fw_ref[0, 1:2, cs] * gate
                    + fw_ref[0, 2:3, cs] * after + fb_ref[0, :, cs])
            act_s[:, cs] = (_silu(conv) * val).astype(BF16)

        def down():
            ffn = jnp.dot(act_s[...], wdn_ref[0], preferred_element_type=F32)
            o_ref[0, rs, :] = x1_s[...] + cur[4][0, 0, 5:6, :] * _rms(ffn, ng_ref[0, 3:4, :])

        return [functools.partial(block, j) for j in range(D_FF // FF_BLOCK)] + [down]

    if carry:
        @pl.when((pl.program_id(0) == 0) & (pl.program_id(1) == 0))
        def _():
            _interleave(mixer_units(cur, 0), [])
    else:
        _interleave(mixer_units(cur, 0), [])
    for r in range(n_sub):
        token[0] = None
        if r + 1 < n_sub:
            ahead = mixer_units(cur, r + 1)
        else:
            ahead = mixer_units(nxt, 0) if carry else []
        _interleave(ffn_units(r), ahead)


def _post(x, cu, hm, so, mods, l, row, p, seg, tm):
    bsz, t, d = x.shape
    n_tiles = t // tm
    sub = min(tm, SUB)
    carry = (tm // sub) % 2 == 0
    tok = lambda n: pl.BlockSpec((1, tm, n), lambda b, i: (b, i, 0))
    acts = [x, cu, hm, so]
    in_specs = [tok(a.shape[-1]) for a in acts] + [_mod_spec(mods, l, row)]
    operands = acts + [mods]
    if carry:
        def ahead(b, i):
            flat = jnp.minimum(b * n_tiles + i + 1, bsz * n_tiles - 1)
            return flat // n_tiles, flat % n_tiles

        def tok_ahead(n):
            def index(b, i):
                nb, ni = ahead(b, i)
                return nb, ni * (tm // sub), 0
            return pl.BlockSpec((1, sub, n), index)

        def mod_ahead(b, i):
            return l, ahead(b, i)[0] if row is None else row, 0, 0

        in_specs += [tok_ahead(a.shape[-1]) for a in acts]
        in_specs += [pl.BlockSpec((1, 1) + mods.shape[2:], mod_ahead)]
        operands += acts + [mods]
    big = [p["w_out"], p["w_up"]]
    return pl.pallas_call(
        functools.partial(_post_kernel, seg, carry),
        grid=(bsz, n_tiles),
        in_specs=in_specs + [_layer_spec(p["head_g"], l), _layer_spec(p["norm_g"], l)]
                 + [_layer_spec(a, l, True) for a in big]
                 + [_layer_spec(p["fw"], l), _layer_spec(p["fb"], l),
                    _layer_spec(p["w_down"], l, True)],
        out_specs=tok(d),
        out_shape=jax.ShapeDtypeStruct((bsz, t, d), F32),
        scratch_shapes=[pltpu.VMEM((sub, D_FF), BF16), pltpu.VMEM((sub, d), F32),
                        pltpu.VMEM((sub, d), BF16)] * 2,
        compiler_params=_params(2),
        name="post",
    )(*operands, p["head_g"], p["norm_g"], *big, p["fw"], p["fb"], p["w_down"])


def _mixer(xs, mods, l, row, p, seg, tm, m0, ct0):
    bsz, t, _ = xs.shape
    cu, q, k, v, so, gt = _inproj(xs, mods, l, row, p, seg, tm)
    rows, rcol, sp, mfin = _gates(gt, m0)
    rows = rows.reshape(bsz, N_ROWQ * N_UNITS, t // PAIR, PAIR)
    hm, ctf = _mlstm(q, k, v, rows, rcol, sp, ct0)
    return cu, hm, so, mfin, ctf


def kernel(x, c, ctx, c_ctx, w_ada, b_ada, norm_g, w_in, b_gates, conv_w, conv_b, conv_ln_g,
           conv_ln_b, mlstm_norm_g, w_out, w_up, ffn_conv_w, ffn_conv_b, w_down):
    bsz, t, d = x.shape
    t_ctx = ctx.shape[1]
    depth = w_ada.shape[0]
    rows = -(-(bsz + 1) // 8) * 8
    cond = jnp.zeros((rows, d), F32).at[:bsz].set(c).at[bsz].set(c_ctx)
    mods = _ada(cond, w_ada, b_ada).reshape(depth, rows, N_MOD, d)

    wg_raw = w_in[:, :, G_OFF:].reshape(depth, d, 2, 2, HEADS)
    bg_raw = b_gates.reshape(depth, 1, 2, 2, HEADS)
    pad = lambda a: jnp.pad(a.reshape(a.shape[0], a.shape[1], N_UNITS),
                            ((0, 0), (0, 0), (0, LANES - N_UNITS)))
    p = {
        "norm_g": norm_g,
        "w_inp": jnp.concatenate([w_in[:, :, :G_OFF], pad(wg_raw[:, :, :, 0]),
                                  pad(wg_raw[:, :, :, 1])], -1).astype(BF16),
        "bg": jnp.concatenate([pad(bg_raw[:, :, :, 0]), pad(bg_raw[:, :, :, 1])], -1),
        "conv_w": conv_w, "conv_b": conv_b[:, None], "ln_g": conv_ln_g[:, None],
        "ln_b": conv_ln_b[:, None], "head_g": mlstm_norm_g[:, None],
        "w_out": w_out.astype(BF16), "w_up": w_up.astype(BF16),
        "fw": ffn_conv_w, "fb": ffn_conv_b[:, None], "w_down": w_down.astype(BF16),
    }
    m_zero = jnp.zeros((bsz, 1, LANES), F32)
    ct_zero = jnp.zeros((bsz, N_UNITS, ST_ROWS, HEAD_DIM), F32)
    tm_in, tm_post = min(TM_INPROJ, t), min(TM_POST, t)
    cs = ctx
    for l in range(depth):
        cu, hm, so, m_c, ct_c = _mixer(cs, mods, l, bsz, p, t_ctx, t_ctx, m_zero, ct_zero)
        if l < depth - 1:
            cs = _post(cs, cu, hm, so, mods, l, bsz, p, t_ctx, t_ctx)
        cu, hm, so, _, _ = _mixer(x, mods, l, None, p, GRID_W, tm_in, m_c, ct_c)
        x = _post(x, cu, hm, so, mods, l, None, p, GRID_W, tm_post)
    return x
```

```python
import functools

import jax
import jax.numpy as jnp
from jax import lax
from jax.experimental import pallas as pl
from jax.experimental.pallas import tpu as pltpu

D_MODEL = 1024
GRID_W = 64
CONV_DIM = 512
CONV_WIDTH = 31
HEADS = 4
HEAD_DIM = 128
MLSTM_DIM = HEADS * HEAD_DIM
N_UNITS = 2 * HEADS
Q_OFF = 2 * CONV_DIM
K_OFF = Q_OFF + MLSTM_DIM
V_OFF = K_OFF + MLSTM_DIM
O_OFF = V_OFF + MLSTM_DIM
G_OFF = O_OFF + MLSTM_DIM
K_SCALE = HEAD_DIM ** -0.5
CHUNK = 64
D_FF = 2816
FF_BLOCK = 256
SUB = 256
TM_INPROJ = 1024
TM_POST = 512
GATE_UNROLL = 4
N_MOD = 6
EPS = 1e-6
LANES = 128
SUBLANES = 8
GATE_PAD = 2 * LANES
VMEM_LIMIT = 56 * 1024 * 1024

F32 = jnp.float32
BF16 = jnp.bfloat16
PAIR = 2 * CHUNK
N_ROWS = 16
ST_ROWS = HEAD_DIM + N_ROWS
ROW_M, ROW_WINTER, ROW_ENEGM, ROW_WSTATE = 0, 1, 2, 3
N_ROWQ = 4


def _params(n_grid):
    return pltpu.CompilerParams(dimension_semantics=("arbitrary",) * n_grid,
                                vmem_limit_bytes=VMEM_LIMIT)


def _rms(x, g):
    return x * lax.rsqrt(jnp.mean(x * x, axis=-1, keepdims=True) + EPS) * g


def _sigmoid(x):
    return 1.0 / (1.0 + jnp.exp(-x))


def _silu(x):
    return x * _sigmoid(x)


def _zero_bits_after(val):
    bits = pltpu.bitcast(val[0:SUBLANES, 0:LANES], jnp.uint32)
    return (bits >> 16) >> 16


def _ordered_after(x, zero_bits):
    if zero_bits is None:
        return x
    tiled = jnp.concatenate([zero_bits] * (x.shape[0] // SUBLANES), axis=0)
    tiled = jnp.concatenate([tiled] * (x.shape[1] // LANES), axis=1)
    return pltpu.bitcast(pltpu.bitcast(x, jnp.uint32) | tiled, x.dtype)


def _interleave(a, b):
    na, nb = len(a), len(b)
    order = sorted([((i + 0.5) / na, 0, i) for i in range(na)]
                   + [((i + 0.5) / nb, 1, i) for i in range(nb)])
    for _, which, i in order:
        (a, b)[which][i]()


def _ada_kernel(c_ref, w_ref, b_ref, o_ref):
    s = _silu(c_ref[...]).astype(BF16)
    o_ref[0] = jnp.dot(s, w_ref[0].astype(BF16), preferred_element_type=F32) + b_ref[0]


def _ada(cond, w_ada, b_ada):
    depth, d, n = w_ada.shape
    rows = cond.shape[0]
    tn = 1536
    return pl.pallas_call(
        _ada_kernel,
        grid=(depth, n // tn),
        in_specs=[pl.BlockSpec((rows, d), lambda l, j: (0, 0)),
                  pl.BlockSpec((1, d, tn), lambda l, j: (l, 0, j)),
                  pl.BlockSpec((1, 1, tn), lambda l, j: (l, 0, j))],
        out_specs=pl.BlockSpec((1, rows, tn), lambda l, j: (l, 0, j)),
        out_shape=jax.ShapeDtypeStruct((depth, rows, n), F32),
        compiler_params=_params(2),
        name="ada",
    )(cond, w_ada, b_ada.reshape(depth, 1, n))


def _inproj_kernel(seg, x_ref, mod_ref, ng_ref, w_ref, wg_ref, bg_ref, cw_ref, cb_ref, lg_ref,
                   lb_ref, cu_ref, q_ref, k_ref, v_ref, so_ref, gt_ref, *scratch):
    tm = x_ref.shape[1]
    sub = min(tm, SUB)
    pos = lax.broadcasted_iota(jnp.int32, (seg, LANES), 0)
    half = CONV_WIDTH // 2
    n_sub = tm // sub
    hb_s = scratch[2 * n_sub]
    token = [None]

    def proj_units(r):
        u_s = scratch[2 * r]
        rs = slice(r * sub, (r + 1) * sub)

        def dot(c0, c1):
            w = wg_ref[0] if c0 == G_OFF else w_ref[0, :, c0:c1]
            res = jnp.dot(hb_s[rs, :], w, preferred_element_type=F32)
            token[0] = _zero_bits_after(res)
            return res

        def norm():
            h = (_rms(x_ref[0, rs, :], ng_ref[0, 0:1, :]) * (1.0 + mod_ref[0, 0, 1:2, :])
                 + mod_ref[0, 0, 0:1, :])
            hb_s[rs, :] = h.astype(BF16)

        def glu():
            u_s[...] = dot(0, CONV_DIM) * _sigmoid(dot(CONV_DIM, Q_OFF))

        def q():
            q_ref[0, rs, :] = dot(Q_OFF, K_OFF).astype(BF16)

        def k():
            k_ref[0, rs, :] = (dot(K_OFF, V_OFF) * K_SCALE).astype(BF16)

        def v():
            v_ref[0, rs, :] = dot(V_OFF, O_OFF).astype(BF16)

        def o():
            so_ref[0, rs, :] = _sigmoid(dot(O_OFF, G_OFF)).astype(BF16)

        def gates():
            gt_ref[0, rs, :] = dot(G_OFF, G_OFF + GATE_PAD) + bg_ref[0]

        return [norm, glu, q, k, v, o, gates]

    def conv_units(r):
        u_s, cv_s = scratch[2 * r], scratch[2 * r + 1]
        rs = slice(r * sub, (r + 1) * sub)

        def conv(ss, ls):
            useg = _ordered_after(u_s[ss, ls], token[0])
            acc = jnp.zeros((seg, LANES), F32) + cb_ref[0, :, ls]
            for j in range(CONV_WIDTH):
                d = j - half
                if d == 0:
                    sh = useg
                else:
                    rolled = pltpu.roll(useg, (-d) % seg, 0)
                    valid = (pos + d >= 0) & (pos + d < seg)
                    sh = jnp.where(valid, rolled, 0.0)
                acc = acc + cw_ref[0, j:j + 1, ls] * sh
            cv_s[ss, ls] = acc

        def norm():
            cv = cv_s[...]
            mu = jnp.mean(cv, axis=-1, keepdims=True)
            var = jnp.mean(jnp.square(cv - mu), axis=-1, keepdims=True)
            y = (cv - mu) * lax.rsqrt(var + EPS) * lg_ref[0] + lb_ref[0]
            cu_ref[0, rs, :] = _silu(y).astype(BF16)

        units = [functools.partial(conv, slice(s * seg, (s + 1) * seg),
                                   slice(lb * LANES, (lb + 1) * LANES))
                 for s in range(sub // seg) for lb in range(CONV_DIM // LANES)]
        return units + [norm]

    for r in range(n_sub + 1):
        _interleave(proj_units(r) if r < n_sub else [], conv_units(r - 1) if r > 0 else [])


def _layer_spec(a, l, single_buffer=False):
    kw = {"pipeline_mode": pl.Buffered(1)} if single_buffer else {}
    return pl.BlockSpec((1,) + a.shape[1:], lambda b, i: (l,) + (0,) * (a.ndim - 1), **kw)


def _mod_spec(mods, l, row):
    return pl.BlockSpec((1, 1) + mods.shape[2:],
                        lambda b, i: (l, b if row is None else row, 0, 0))


def _inproj(x, mods, l, row, p, seg, tm):
    bsz, t, d = x.shape
    tok = lambda n: pl.BlockSpec((1, tm, n), lambda b, i: (b, i, 0))
    act = lambda n, dt: jax.ShapeDtypeStruct((bsz, t, n), dt)
    small = [p["w_gate"], p["bg"], p["conv_w"], p["conv_b"], p["ln_g"], p["ln_b"]]
    return pl.pallas_call(
        functools.partial(_inproj_kernel, seg),
        grid=(bsz, t // tm),
        in_specs=[tok(d), _mod_spec(mods, l, row), _layer_spec(p["norm_g"], l),
                  _layer_spec(p["w_inp"], l, True)] + [_layer_spec(a, l) for a in small],
        out_specs=[tok(CONV_DIM), tok(MLSTM_DIM), tok(MLSTM_DIM), tok(MLSTM_DIM), tok(MLSTM_DIM),
                   tok(GATE_PAD)],
        out_shape=[act(CONV_DIM, BF16), act(MLSTM_DIM, BF16), act(MLSTM_DIM, BF16),
                   act(MLSTM_DIM, BF16), act(MLSTM_DIM, BF16), act(GATE_PAD, F32)],
        scratch_shapes=([pltpu.VMEM((min(tm, SUB), CONV_DIM), F32)] * (2 * (-(-tm // SUB)))
                        + [pltpu.VMEM((tm, d), BF16)]),
        compiler_params=_params(2),
        name="inproj",
    )(x, mods, p["norm_g"], p["w_inp"], *small)


def _gate_kernel(gt_ref, m0_ref, rows_ref, rcol_ref, sp_ref, mfin_ref,
                 bcum_s, cm_s, q0_s, q1_s, q2_s, q3_s, bl_s, ml_s, mina_s, minb_s, mnewa_s,
                 mnewb_s):
    t = gt_ref.shape[1]
    nc = t // CHUNK
    lane = lax.broadcasted_iota(jnp.int32, (1, LANES), 1)
    fwd = (lane % N_UNITS) < HEADS
    pos = lax.broadcasted_iota(jnp.int32, (CHUNK, LANES), 0)

    def scan(x, op, ident):
        xf, xb = x, x
        k = 1
        while k < CHUNK:
            xf = op(xf, jnp.where(pos >= k, pltpu.roll(xf, k, 0), ident))
            xb = op(xb, jnp.where(pos < CHUNK - k, pltpu.roll(xb, CHUNK - k, 0), ident))
            k *= 2
        return jnp.where(fwd, xf, xb)

    def prep(c, carry):
        r0 = pl.multiple_of(c * CHUNK, CHUNK)
        gi = gt_ref[0, pl.ds(r0, CHUNK), 0:LANES]
        gf = gt_ref[0, pl.ds(r0, CHUNK), LANES:GATE_PAD]
        logf = jnp.minimum(gf, 0.0) - jnp.log1p(jnp.exp(-jnp.abs(gf)))
        bcum = scan(logf, jnp.add, 0.0)
        r = gi - bcum
        bcum_s[pl.ds(r0, CHUNK), :] = bcum
        rcol_ref[0, pl.ds(r0, CHUNK), :] = r
        cm_s[pl.ds(r0, CHUNK), :] = scan(r, jnp.maximum, -jnp.inf)
        bl = jnp.sum(logf, axis=0, keepdims=True)
        bl_s[pl.ds(c, 1), :] = bl
        ml_s[pl.ds(c, 1), :] = bl + jnp.max(r, axis=0, keepdims=True)
        return carry

    lax.fori_loop(0, nc, prep, 0, unroll=GATE_UNROLL)

    def mscan(i, m):
        j = nc - 1 - i
        bl = jnp.where(fwd, bl_s[pl.ds(i, 1), :], bl_s[pl.ds(j, 1), :])
        ml = jnp.where(fwd, ml_s[pl.ds(i, 1), :], ml_s[pl.ds(j, 1), :])
        m_new = jnp.maximum(bl + m, ml)
        mina_s[pl.ds(i, 1), :] = m
        minb_s[pl.ds(j, 1), :] = m
        mnewa_s[pl.ds(i, 1), :] = m_new
        mnewb_s[pl.ds(j, 1), :] = m_new
        return m_new

    mfin_ref[0] = lax.fori_loop(0, nc, mscan, m0_ref[0])

    m_in_all = jnp.where(fwd, mina_s[...], minb_s[...])
    m_new_all = jnp.where(fwd, mnewa_s[...], mnewb_s[...])
    sprev = jnp.exp(bl_s[...] + m_in_all - m_new_all)
    for u in range(N_UNITS):
        sp_ref[0, u] = jnp.broadcast_to(sprev[:, u:u + 1], (nc, LANES))

    def finish(c, carry):
        r0 = pl.multiple_of(c * CHUNK, CHUNK)
        m_in = jnp.where(fwd, mina_s[pl.ds(c, 1), :], minb_s[pl.ds(c, 1), :])
        m_new = jnp.where(fwd, mnewa_s[pl.ds(c, 1), :], mnewb_s[pl.ds(c, 1), :])
        big_m = jnp.maximum(m_in, cm_s[pl.ds(r0, CHUNK), :])
        q0_s[pl.ds(r0, CHUNK), :] = big_m
        q1_s[pl.ds(r0, CHUNK), :] = jnp.exp(m_in - big_m)
        q2_s[pl.ds(r0, CHUNK), :] = jnp.exp(-(bcum_s[pl.ds(r0, CHUNK), :] + big_m))
        q3_s[pl.ds(r0, CHUNK), :] = jnp.exp(
            bl_s[pl.ds(c, 1), :] + rcol_ref[0, pl.ds(r0, CHUNK), :] - m_new)
        return carry

    lax.fori_loop(0, nc, finish, 0, unroll=GATE_UNROLL)

    for blk in range(t // LANES):
        bs = slice(blk * LANES, (blk + 1) * LANES)
        for k, src in enumerate((q0_s, q1_s, q2_s, q3_s)):
            rows_ref[0, k * N_UNITS:(k + 1) * N_UNITS, bs] = src[bs, :].T[0:N_UNITS, :]


def _gates(gt, m0):
    bsz, t, _ = gt.shape
    nc = t // CHUNK
    tl = lambda: pltpu.VMEM((t, LANES), F32)
    cl = lambda: pltpu.VMEM((nc, LANES), F32)
    return pl.pallas_call(
        _gate_kernel,
        grid=(bsz,),
        in_specs=[pl.BlockSpec((1, t, GATE_PAD), lambda b: (b, 0, 0)),
                  pl.BlockSpec((1, 1, LANES), lambda b: (b, 0, 0))],
        out_specs=[pl.BlockSpec((1, N_ROWQ * N_UNITS, t), lambda b: (b, 0, 0)),
                   pl.BlockSpec((1, t, LANES), lambda b: (b, 0, 0)),
                   pl.BlockSpec((1, N_UNITS, nc, LANES), lambda b: (b, 0, 0, 0)),
                   pl.BlockSpec((1, 1, LANES), lambda b: (b, 0, 0))],
        out_shape=[jax.ShapeDtypeStruct((bsz, N_ROWQ * N_UNITS, t), F32),
                   jax.ShapeDtypeStruct((bsz, t, LANES), F32),
                   jax.ShapeDtypeStruct((bsz, N_UNITS, nc, LANES), F32),
                   jax.ShapeDtypeStruct((bsz, 1, LANES), F32)],
        scratch_shapes=[tl() for _ in range(6)] + [cl() for _ in range(6)],
        compiler_params=_params(1),
        name="gates",
    )(gt, m0)


def _mlstm_kernel(q_ref, k_ref, v_ref, rows_ref, rcol_ref, sp_ref, c0_ref, h_ref, cf_ref,
                  vt_s, cin_s, st_s):
    t = q_ref.shape[1]
    n_pairs = t // PAIR
    row = lax.broadcasted_iota(jnp.int32, (PAIR, PAIR), 0)
    col = lax.broadcasted_iota(jnp.int32, (PAIR, PAIR), 1)
    same = (row >= CHUNK) == (col >= CHUNK)
    masks = (same & (col >= row), same & (col <= row))
    lane = lax.broadcasted_iota(jnp.int32, (1, PAIR), 1)
    halves = (lane < CHUNK, lane >= CHUNK)
    qrow = lax.broadcasted_iota(jnp.int32, (PAIR, HEAD_DIM), 0)
    nt_dims = (((1,), (1,)), ((), ()))
    heads = [slice(hd * HEAD_DIM, (hd + 1) * HEAD_DIM) for hd in range(HEADS)]

    def rowq(kind, u, p):
        return rows_ref[0, kind * N_UNITS + u, pl.ds(p, 1), :]

    for hd, hs in enumerate(heads):
        for blk in range(n_pairs):
            vt_s[hd, blk, 0:HEAD_DIM, :] = (
                v_ref[0, blk * PAIR:(blk + 1) * PAIR, hs].astype(F32).T.astype(BF16))
            vt_s[hd, blk, HEAD_DIM:, :] = jnp.ones((N_ROWS, PAIR), BF16)
    st_s[...] = c0_ref[0]

    def state_body(i, carry):
        for direction in range(2):
            p = i if direction == 0 else n_pairs - 1 - i
            r0 = pl.multiple_of(p * PAIR, PAIR)
            for hd, hs in enumerate(heads):
                u = direction * HEADS + hd
                vw = vt_s[hd, p].astype(F32) * rowq(ROW_WSTATE, u, p)
                lhs = jnp.concatenate([jnp.where(halves[0], vw, 0.0).astype(BF16),
                                       jnp.where(halves[1], vw, 0.0).astype(BF16)], axis=0)
                loc = jnp.dot(lhs, k_ref[0, pl.ds(r0, PAIR), hs], preferred_element_type=F32)
                st = st_s[u]
                for j in ((0, 1) if direction == 0 else (1, 0)):
                    cin_s[hd, p, direction * ST_ROWS:(direction + 1) * ST_ROWS,
                          j * HEAD_DIM:(j + 1) * HEAD_DIM] = st.astype(BF16)
                    st = (st * sp_ref[0, u, pl.ds(2 * p + j, 1), :]
                          + loc[j * ST_ROWS:(j + 1) * ST_ROWS])
                st_s[u] = st
        return carry

    lax.fori_loop(0, n_pairs, state_body, 0, unroll=2)
    cf_ref[0] = st_s[...]

    n_items = HEADS

    def out_body(p, carry):
        val = [dict() for _ in range(n_items)]
        r0 = pl.multiple_of(p * PAIR, PAIR)
        rcol = rcol_ref[0, pl.ds(r0, PAIR), :]

        def row_after(x, zero_bits):
            if zero_bits is None:
                return x
            return pltpu.bitcast(pltpu.bitcast(x, jnp.uint32) | zero_bits[0:1], x.dtype)

        def where(item):
            return p, r0, rcol, item

        def scores(item):
            p, r0, rcol, hd = where(item)
            hs = heads[hd]
            qp = q_ref[0, pl.ds(r0, PAIR), hs]
            val[item]["s_t"] = lax.dot_general(k_ref[0, pl.ds(r0, PAIR), hs], qp, nt_dims,
                                               preferred_element_type=F32)
            q_blk = jnp.concatenate([jnp.where(qrow < CHUNK, qp, 0),
                                     jnp.where(qrow >= CHUNK, qp, 0)], axis=1)
            inter = lax.dot_general(cin_s[hd, p], q_blk, nt_dims,
                                    preferred_element_type=F32)
            val[item]["inter"] = inter
            val[item]["scores_done"] = _zero_bits_after(inter)

        def weights(item):
            p, r0, rcol, hd = where(item)
            nxt = val[item + 1]["scores_done"] if item + 1 < n_items else None
            pts = []
            for direction in range(2):
                u = direction * HEADS + hd
                w = jnp.exp(rcol[:, u:u + 1] - row_after(rowq(ROW_M, u, p), nxt))
                pts.append((val[item]["s_t"] * jnp.where(masks[direction], w, 0.0)).astype(BF16))
            val[item]["pt"] = jnp.concatenate(pts, axis=1)

        def intra(item):
            p, r0, rcol, hd = where(item)
            res = jnp.dot(vt_s[hd, p], val[item]["pt"], preferred_element_type=F32)
            val[item]["intra"] = res
            val[item]["intra_done"] = _zero_bits_after(res)

        def combine(item):
            p, r0, rcol, hd = where(item)
            nxt = val[item + 1]["intra_done"] if item + 1 < n_items else None
            h_t = jnp.zeros((HEAD_DIM, PAIR), F32)
            for direction in range(2):
                u = direction * HEADS + hd
                tot = (val[item]["intra"][:, direction * PAIR:(direction + 1) * PAIR]
                       + row_after(rowq(ROW_WINTER, u, p), nxt)
                       * val[item]["inter"][direction * ST_ROWS:(direction + 1) * ST_ROWS])
                den = jnp.maximum(jnp.abs(tot[HEAD_DIM:HEAD_DIM + 1]), rowq(ROW_ENEGM, u, p))
                h_t = h_t + tot[0:HEAD_DIM] * (1.0 / den)
            h_ref[0, pl.ds(r0, PAIR), heads[hd]] = h_t.T

        for stage, item in ((scores, 0), (scores, 1), (scores, 2), (scores, 3), (weights, 0),
                            (weights, 1), (intra, 0), (weights, 2), (intra, 1), (weights, 3),
                            (intra, 2), (combine, 0), (intra, 3), (combine, 1), (combine, 2),
                            (combine, 3)):
            stage(item)
        return carry

    lax.fori_loop(0, n_pairs, out_body, 0)


def _mlstm(q, k, v, rows, rcol, sp, c0):
    bsz, t, _ = q.shape
    nc = t // CHUNK
    n_pairs = t // PAIR
    tok = pl.BlockSpec((1, t, MLSTM_DIM), lambda b: (b, 0, 0))
    st = pl.BlockSpec((1, N_UNITS, ST_ROWS, HEAD_DIM), lambda b: (b, 0, 0, 0))
    return pl.pallas_call(
        _mlstm_kernel,
        grid=(bsz,),
        in_specs=[tok, tok, tok,
                  pl.BlockSpec((1, N_ROWQ * N_UNITS, n_pairs, PAIR), lambda b: (b, 0, 0, 0)),
                  pl.BlockSpec((1, t, LANES), lambda b: (b, 0, 0)),
                  pl.BlockSpec((1, N_UNITS, nc, LANES), lambda b: (b, 0, 0, 0)),
                  st],
        out_specs=[tok, st],
        out_shape=[jax.ShapeDtypeStruct((bsz, t, MLSTM_DIM), F32),
                   jax.ShapeDtypeStruct((bsz, N_UNITS, ST_ROWS, HEAD_DIM), F32)],
        scratch_shapes=[pltpu.VMEM((HEADS, n_pairs, ST_ROWS, PAIR), BF16),
                        pltpu.VMEM((HEADS, n_pairs, 2 * ST_ROWS, 2 * HEAD_DIM), BF16),
                        pltpu.VMEM((N_UNITS, ST_ROWS, HEAD_DIM), F32)],
        compiler_params=_params(1),
        name="mlstm",
    )(q, k, v, rows, rcol, sp, c0)


def _post_kernel(seg, x_ref, cu_ref, hm_ref, so_ref, mod_ref, hg_ref, ng_ref, wo_ref, wup_ref,
                 fw_ref, fb_ref, wdn_ref, o_ref, *scratch):
    tm = x_ref.shape[1]
    sub = min(tm, SUB)
    pos = lax.broadcasted_iota(jnp.int32, (sub, FF_BLOCK), 0) % seg
    first = pos == 0
    last = pos == seg - 1
    n_sub = tm // sub
    token = [None]

    def mixer_units(r):
        rs = slice(r * sub, (r + 1) * sub)
        x1_s, h2_s = scratch[3 * r + 1], scratch[3 * r + 2]
        y = [None]

        def outproj():
            hm = hm_ref[0, rs, :]
            so = so_ref[0, rs, :].astype(F32)
            parts = []
            for hd in range(HEADS):
                hs = slice(hd * HEAD_DIM, (hd + 1) * HEAD_DIM)
                hh = hm[:, hs]
                mu = jnp.mean(hh, axis=-1, keepdims=True)
                var = jnp.mean(jnp.square(hh - mu), axis=-1, keepdims=True)
                parts.append(so[:, hs] * ((hh - mu) * lax.rsqrt(var + EPS) * hg_ref[0, :, hs]))
            m = jnp.concatenate(parts, axis=-1).astype(BF16)
            y[0] = (jnp.dot(cu_ref[0, rs, :], wo_ref[0, 0:CONV_DIM, :], preferred_element_type=F32)
                    + jnp.dot(m, wo_ref[0, CONV_DIM:, :], preferred_element_type=F32))
            token[0] = _zero_bits_after(y[0])

        def residual():
            x1 = x_ref[0, rs, :] + mod_ref[0, 0, 2:3, :] * _rms(y[0], ng_ref[0, 1:2, :])
            x1_s[...] = x1
            h2_s[...] = (_rms(x1, ng_ref[0, 2:3, :]) * (1.0 + mod_ref[0, 0, 4:5, :])
                         + mod_ref[0, 0, 3:4, :]).astype(BF16)
            token[0] = _zero_bits_after(x1)

        return [outproj, residual]

    def ffn_units(r):
        rs = slice(r * sub, (r + 1) * sub)
        act_s, x1_s, h2_s = scratch[3 * r], scratch[3 * r + 1], scratch[3 * r + 2]

        def block(j):
            cs = slice(j * FF_BLOCK, (j + 1) * FF_BLOCK)
            gs = slice(D_FF + j * FF_BLOCK, D_FF + (j + 1) * FF_BLOCK)
            val = jnp.dot(h2_s[...], wup_ref[0, :, cs], preferred_element_type=F32)
            gate = jnp.dot(h2_s[...], wup_ref[0, :, gs], preferred_element_type=F32)
            gate = _ordered_after(gate, token[0])
            before = jnp.where(first, 0.0, pltpu.roll(gate, 1, 0))
            after = jnp.where(last, 0.0, pltpu.roll(gate, sub - 1, 0))
            conv = (fw_ref[0, 0:1, cs] * before + fw_ref[0, 1:2, cs] * gate
                    + fw_ref[0, 2:3, cs] * after + fb_ref[0, :, cs])
            act_s[:, cs] = (_silu(conv) * val).astype(BF16)

        def down():
            ffn = jnp.dot(act_s[...], wdn_ref[0], preferred_element_type=F32)
            o_ref[0, rs, :] = x1_s[...] + mod_ref[0, 0, 5:6, :] * _rms(ffn, ng_ref[0, 3:4, :])

        return [functools.partial(block, j) for j in range(D_FF // FF_BLOCK)] + [down]

    _interleave(mixer_units(0), [])
    for r in range(n_sub):
        token[0] = None
        _interleave(ffn_units(r), mixer_units(r + 1) if r + 1 < n_sub else [])


def _post(x, cu, hm, so, mods, l, row, p, seg, tm):
    bsz, t, d = x.shape
    sub = min(tm, SUB)
    tok = lambda n: pl.BlockSpec((1, tm, n), lambda b, i: (b, i, 0))
    acts = [x, cu, hm, so]
    big = [p["w_out"], p["w_up"]]
    return pl.pallas_call(
        functools.partial(_post_kernel, seg),
        grid=(bsz, t // tm),
        in_specs=[tok(a.shape[-1]) for a in acts]
                 + [_mod_spec(mods, l, row), _layer_spec(p["head_g"], l),
                    _layer_spec(p["norm_g"], l)]
                 + [_layer_spec(a, l, True) for a in big]
                 + [_layer_spec(p["fw"], l), _layer_spec(p["fb"], l),
                    _layer_spec(p["w_down"], l, True)],
        out_specs=tok(d),
        out_shape=jax.ShapeDtypeStruct((bsz, t, d), F32),
        scratch_shapes=[pltpu.VMEM((sub, D_FF), BF16), pltpu.VMEM((sub, d), F32),
                        pltpu.VMEM((sub, d), BF16)] * (tm // sub),
        compiler_params=_params(2),
        name="post",
    )(*acts, mods, p["head_g"], p["norm_g"], *big, p["fw"], p["fb"], p["w_down"])


def _mixer(xs, mods, l, row, p, seg, tm, m0, ct0):
    bsz, t, _ = xs.shape
    cu, q, k, v, so, gt = _inproj(xs, mods, l, row, p, seg, tm)
    rows, rcol, sp, mfin = _gates(gt, m0)
    rows = rows.reshape(bsz, N_ROWQ * N_UNITS, t // PAIR, PAIR)
    hm, ctf = _mlstm(q, k, v, rows, rcol, sp, ct0)
    return cu, hm, so, mfin, ctf


def kernel(x, c, ctx, c_ctx, w_ada, b_ada, norm_g, w_in, b_gates, conv_w, conv_b, conv_ln_g,
           conv_ln_b, mlstm_norm_g, w_out, w_up, ffn_conv_w, ffn_conv_b, w_down):
    bsz, t, d = x.shape
    t_ctx = ctx.shape[1]
    depth = w_ada.shape[0]
    rows = -(-(bsz + 1) // 8) * 8
    cond = jnp.zeros((rows, d), F32).at[:bsz].set(c).at[bsz].set(c_ctx)
    mods = _ada(cond, w_ada, b_ada).reshape(depth, rows, N_MOD, d)

    wg_raw = w_in[:, :, G_OFF:].reshape(depth, d, 2, 2, HEADS)
    bg_raw = b_gates.reshape(depth, 1, 2, 2, HEADS)
    pad = lambda a: jnp.pad(a.reshape(a.shape[0], a.shape[1], N_UNITS),
                            ((0, 0), (0, 0), (0, LANES - N_UNITS)))
    p = {
        "norm_g": norm_g,
        "w_inp": w_in.astype(BF16),
        "w_gate": jnp.concatenate([pad(wg_raw[:, :, :, 0]), pad(wg_raw[:, :, :, 1])],
                                  -1).astype(BF16),
        "bg": jnp.concatenate([pad(bg_raw[:, :, :, 0]), pad(bg_raw[:, :, :, 1])], -1),
        "conv_w": conv_w, "conv_b": conv_b[:, None], "ln_g": conv_ln_g[:, None],
        "ln_b": conv_ln_b[:, None], "head_g": mlstm_norm_g[:, None],
        "w_out": w_out.astype(BF16), "w_up": w_up.astype(BF16),
        "fw": ffn_conv_w, "fb": ffn_conv_b[:, None], "w_down": w_down.astype(BF16),
    }
    m_zero = jnp.zeros((bsz, 1, LANES), F32)
    ct_zero = jnp.zeros((bsz, N_UNITS, ST_ROWS, HEAD_DIM), F32)
    tm_in, tm_post = min(TM_INPROJ, t), min(TM_POST, t)
    cs = ctx
    for l in range(depth):
        cu, hm, so, m_c, ct_c = _mixer(cs, mods, l, bsz, p, t_ctx, t_ctx, m_zero, ct_zero)
        if l < depth - 1:
            cs = _post(cs, cu, hm, so, mods, l, bsz, p, t_ctx, t_ctx)
        cu, hm, so, _, _ = _mixer(x, mods, l, None, p, GRID_W, tm_in, m_c, ct_c)
        x = _post(x, cu, hm, so, mods, l, None, p, GRID_W, tm_post)
    return x
```

```python
import functools

import jax
import jax.numpy as jnp
from jax import lax
from jax.experimental import pallas as pl
from jax.experimental.pallas import tpu as pltpu

D_MODEL = 1024
GRID_W = 64
CONV_DIM = 512
CONV_WIDTH = 31
HEADS = 4
HEAD_DIM = 128
MLSTM_DIM = HEADS * HEAD_DIM
N_UNITS = 2 * HEADS
Q_OFF = 2 * CONV_DIM
K_OFF = Q_OFF + MLSTM_DIM
V_OFF = K_OFF + MLSTM_DIM
O_OFF = V_OFF + MLSTM_DIM
G_OFF = O_OFF + MLSTM_DIM
K_SCALE = HEAD_DIM ** -0.5
CHUNK = 64
D_FF = 2816
FF_BLOCK = 256
SUB = 256
TM_INPROJ = 1024
TM_POST = 512
GATE_UNROLL = 4
N_MOD = 6
EPS = 1e-6
LANES = 128
SUBLANES = 8
GATE_PAD = 2 * LANES
VMEM_LIMIT = 56 * 1024 * 1024

F32 = jnp.float32
BF16 = jnp.bfloat16
PAIR = 2 * CHUNK
N_ROWS = 16
ST_ROWS = HEAD_DIM + N_ROWS
ROW_M, ROW_WINTER, ROW_ENEGM, ROW_WSTATE = 0, 1, 2, 3
N_ROWQ = 4


def _params(n_grid):
    return pltpu.CompilerParams(dimension_semantics=("arbitrary",) * n_grid,
                                vmem_limit_bytes=VMEM_LIMIT)


def _rms(x, g):
    return x * lax.rsqrt(jnp.mean(x * x, axis=-1, keepdims=True) + EPS) * g


def _sigmoid(x):
    return 1.0 / (1.0 + jnp.exp(-x))


def _silu(x):
    return x * _sigmoid(x)


def _zero_bits_after(val):
    bits = pltpu.bitcast(val[0:SUBLANES, 0:LANES], jnp.uint32)
    return (bits >> 16) >> 16


def _ordered_after(x, zero_bits):
    if zero_bits is None:
        return x
    tiled = jnp.concatenate([zero_bits] * (x.shape[0] // SUBLANES), axis=0)
    tiled = jnp.concatenate([tiled] * (x.shape[1] // LANES), axis=1)
    return pltpu.bitcast(pltpu.bitcast(x, jnp.uint32) | tiled, x.dtype)


def _interleave(a, b):
    na, nb = len(a), len(b)
    order = sorted([((i + 0.5) / na, 0, i) for i in range(na)]
                   + [((i + 0.5) / nb, 1, i) for i in range(nb)])
    for _, which, i in order:
        (a, b)[which][i]()


def _ada_kernel(c_ref, w_ref, b_ref, o_ref):
    s = _silu(c_ref[...]).astype(BF16)
    o_ref[0] = jnp.dot(s, w_ref[0].astype(BF16), preferred_element_type=F32) + b_ref[0]


def _ada(cond, w_ada, b_ada):
    depth, d, n = w_ada.shape
    rows = cond.shape[0]
    tn = 1536
    return pl.pallas_call(
        _ada_kernel,
        grid=(depth, n // tn),
        in_specs=[pl.BlockSpec((rows, d), lambda l, j: (0, 0)),
                  pl.BlockSpec((1, d, tn), lambda l, j: (l, 0, j)),
                  pl.BlockSpec((1, 1, tn), lambda l, j: (l, 0, j))],
        out_specs=pl.BlockSpec((1, rows, tn), lambda l, j: (l, 0, j)),
        out_shape=jax.ShapeDtypeStruct((depth, rows, n), F32),
        compiler_params=_params(2),
        name="ada",
    )(cond, w_ada, b_ada.reshape(depth, 1, n))


def _inproj_kernel(seg, x_ref, mod_ref, ng_ref, w_ref, wg_ref, bg_ref, cw_ref, cb_ref, lg_ref,
                   lb_ref, cu_ref, q_ref, k_ref, v_ref, so_ref, gt_ref, *scratch):
    tm = x_ref.shape[1]
    sub = min(tm, SUB)
    pos = lax.broadcasted_iota(jnp.int32, (seg, LANES), 0)
    half = CONV_WIDTH // 2
    n_sub = tm // sub
    hb_s = scratch[2 * n_sub]
    token = [None]

    def proj_units(r):
        u_s = scratch[2 * r]
        rs = slice(r * sub, (r + 1) * sub)

        def dot(c0, c1):
            w = wg_ref[0] if c0 == G_OFF else w_ref[0, :, c0:c1]
            res = jnp.dot(hb_s[rs, :], w, preferred_element_type=F32)
            token[0] = _zero_bits_after(res)
            return res

        def norm():
            h = (_rms(x_ref[0, rs, :], ng_ref[0, 0:1, :]) * (1.0 + mod_ref[0, 0, 1:2, :])
                 + mod_ref[0, 0, 0:1, :])
            hb_s[rs, :] = h.astype(BF16)

        def glu():
            u_s[...] = dot(0, CONV_DIM) * _sigmoid(dot(CONV_DIM, Q_OFF))

        def q():
            q_ref[0, rs, :] = dot(Q_OFF, K_OFF).astype(BF16)

        def k():
            k_ref[0, rs, :] = (dot(K_OFF, V_OFF) * K_SCALE).astype(BF16)

        def v():
            v_ref[0, rs, :] = dot(V_OFF, O_OFF).astype(BF16)

        def o():
            so_ref[0, rs, :] = _sigmoid(dot(O_OFF, G_OFF)).astype(BF16)

        def gates():
            gt_ref[0, rs, :] = dot(G_OFF, G_OFF + GATE_PAD) + bg_ref[0]

        return [norm, glu, q, k, v, o, gates]

    def conv_units(r):
        u_s, cv_s = scratch[2 * r], scratch[2 * r + 1]
        rs = slice(r * sub, (r + 1) * sub)

        def conv(ss, ls):
            useg = _ordered_after(u_s[ss, ls], token[0])
            acc = jnp.zeros((seg, LANES), F32) + cb_ref[0, :, ls]
            for j in range(CONV_WIDTH):
                d = j - half
                if d == 0:
                    sh = useg
                else:
                    rolled = pltpu.roll(useg, (-d) % seg, 0)
                    valid = (pos + d >= 0) & (pos + d < seg)
                    sh = jnp.where(valid, rolled, 0.0)
                acc = acc + cw_ref[0, j:j + 1, ls] * sh
            cv_s[ss, ls] = acc

        def norm():
            cv = cv_s[...]
            mu = jnp.mean(cv, axis=-1, keepdims=True)
            var = jnp.mean(jnp.square(cv - mu), axis=-1, keepdims=True)
            y = (cv - mu) * lax.rsqrt(var + EPS) * lg_ref[0] + lb_ref[0]
            cu_ref[0, rs, :] = _silu(y).astype(BF16)

        units = [functools.partial(conv, slice(s * seg, (s + 1) * seg),
                                   slice(lb * LANES, (lb + 1) * LANES))
                 for s in range(sub // seg) for lb in range(CONV_DIM // LANES)]
        return units + [norm]

    for r in range(n_sub + 1):
        _interleave(proj_units(r) if r < n_sub else [], conv_units(r - 1) if r > 0 else [])


def _layer_spec(a, l, single_buffer=False):
    kw = {"pipeline_mode": pl.Buffered(1)} if single_buffer else {}
    return pl.BlockSpec((1,) + a.shape[1:], lambda b, i: (l,) + (0,) * (a.ndim - 1), **kw)


def _mod_spec(mods, l, row):
    return pl.BlockSpec((1, 1) + mods.shape[2:],
                        lambda b, i: (l, b if row is None else row, 0, 0))


def _inproj(x, mods, l, row, p, seg, tm):
    bsz, t, d = x.shape
    tok = lambda n: pl.BlockSpec((1, tm, n), lambda b, i: (b, i, 0))
    act = lambda n, dt: jax.ShapeDtypeStruct((bsz, t, n), dt)
    small = [p["w_gate"], p["bg"], p["conv_w"], p["conv_b"], p["ln_g"], p["ln_b"]]
    return pl.pallas_call(
        functools.partial(_inproj_kernel, seg),
        grid=(bsz, t // tm),
        in_specs=[tok(d), _mod_spec(mods, l, row), _layer_spec(p["norm_g"], l),
                  _layer_spec(p["w_inp"], l, True)] + [_layer_spec(a, l) for a in small],
        out_specs=[tok(CONV_DIM), tok(MLSTM_DIM), tok(MLSTM_DIM), tok(MLSTM_DIM), tok(MLSTM_DIM),
                   tok(GATE_PAD)],
        out_shape=[act(CONV_DIM, BF16), act(MLSTM_DIM, BF16), act(MLSTM_DIM, BF16),
                   act(MLSTM_DIM, BF16), act(MLSTM_DIM, BF16), act(GATE_PAD, F32)],
        scratch_shapes=([pltpu.VMEM((min(tm, SUB), CONV_DIM), F32)] * (2 * (-(-tm // SUB)))
                        + [pltpu.VMEM((tm, d), BF16)]),
        compiler_params=_params(2),
        name="inproj",
    )(x, mods, p["norm_g"], p["w_inp"], *small)


def _gate_kernel(gt_ref, m0_ref, rows_ref, rcol_ref, sp_ref, mfin_ref,
                 bcum_s, cm_s, q0_s, q1_s, q2_s, q3_s, bl_s, ml_s, mina_s, minb_s, mnewa_s,
                 mnewb_s):
    t = gt_ref.shape[1]
    nc = t // CHUNK
    lane = lax.broadcasted_iota(jnp.int32, (1, LANES), 1)
    fwd = (lane % N_UNITS) < HEADS
    pos = lax.broadcasted_iota(jnp.int32, (CHUNK, LANES), 0)

    def scan(x, op, ident):
        xf, xb = x, x
        k = 1
        while k < CHUNK:
            xf = op(xf, jnp.where(pos >= k, pltpu.roll(xf, k, 0), ident))
            xb = op(xb, jnp.where(pos < CHUNK - k, pltpu.roll(xb, CHUNK - k, 0), ident))
            k *= 2
        return jnp.where(fwd, xf, xb)

    def prep(c, carry):
        r0 = pl.multiple_of(c * CHUNK, CHUNK)
        gi = gt_ref[0, pl.ds(r0, CHUNK), 0:LANES]
        gf = gt_ref[0, pl.ds(r0, CHUNK), LANES:GATE_PAD]
        logf = jnp.minimum(gf, 0.0) - jnp.log1p(jnp.exp(-jnp.abs(gf)))
        bcum = scan(logf, jnp.add, 0.0)
        r = gi - bcum
        bcum_s[pl.ds(r0, CHUNK), :] = bcum
        rcol_ref[0, pl.ds(r0, CHUNK), :] = r
        cm_s[pl.ds(r0, CHUNK), :] = scan(r, jnp.maximum, -jnp.inf)
        bl = jnp.sum(logf, axis=0, keepdims=True)
        bl_s[pl.ds(c, 1), :] = bl
        ml_s[pl.ds(c, 1), :] = bl + jnp.max(r, axis=0, keepdims=True)
        return carry

    lax.fori_loop(0, nc, prep, 0, unroll=GATE_UNROLL)

    def mscan(i, m):
        j = nc - 1 - i
        bl = jnp.where(fwd, bl_s[pl.ds(i, 1), :], bl_s[pl.ds(j, 1), :])
        ml = jnp.where(fwd, ml_s[pl.ds(i, 1), :], ml_s[pl.ds(j, 1), :])
        m_new = jnp.maximum(bl + m, ml)
        mina_s[pl.ds(i, 1), :] = m
        minb_s[pl.ds(j, 1), :] = m
        mnewa_s[pl.ds(i, 1), :] = m_new
        mnewb_s[pl.ds(j, 1), :] = m_new
        return m_new

    mfin_ref[0] = lax.fori_loop(0, nc, mscan, m0_ref[0])

    m_in_all = jnp.where(fwd, mina_s[...], minb_s[...])
    m_new_all = jnp.where(fwd, mnewa_s[...], mnewb_s[...])
    sprev = jnp.exp(bl_s[...] + m_in_all - m_new_all)
    for u in range(N_UNITS):
        sp_ref[0, u] = jnp.broadcast_to(sprev[:, u:u + 1], (nc, LANES))

    def finish(c, carry):
        r0 = pl.multiple_of(c * CHUNK, CHUNK)
        m_in = jnp.where(fwd, mina_s[pl.ds(c, 1), :], minb_s[pl.ds(c, 1), :])
        m_new = jnp.where(fwd, mnewa_s[pl.ds(c, 1), :], mnewb_s[pl.ds(c, 1), :])
        big_m = jnp.maximum(m_in, cm_s[pl.ds(r0, CHUNK), :])
        q0_s[pl.ds(r0, CHUNK), :] = big_m
        q1_s[pl.ds(r0, CHUNK), :] = jnp.exp(m_in - big_m)
        q2_s[pl.ds(r0, CHUNK), :] = jnp.exp(-(bcum_s[pl.ds(r0, CHUNK), :] + big_m))
        q3_s[pl.ds(r0, CHUNK), :] = jnp.exp(
            bl_s[pl.ds(c, 1), :] + rcol_ref[0, pl.ds(r0, CHUNK), :] - m_new)
        return carry

    lax.fori_loop(0, nc, finish, 0, unroll=GATE_UNROLL)

    for blk in range(t // LANES):
        bs = slice(blk * LANES, (blk + 1) * LANES)
        for k, src in enumerate((q0_s, q1_s, q2_s, q3_s)):
            rows_ref[0, k * N_UNITS:(k + 1) * N_UNITS, bs] = src[bs, :].T[0:N_UNITS, :]


def _gates(gt, m0):
    bsz, t, _ = gt.shape
    nc = t // CHUNK
    tl = lambda: pltpu.VMEM((t, LANES), F32)
    cl = lambda: pltpu.VMEM((nc, LANES), F32)
    return pl.pallas_call(
        _gate_kernel,
        grid=(bsz,),
        in_specs=[pl.BlockSpec((1, t, GATE_PAD), lambda b: (b, 0, 0)),
                  pl.BlockSpec((1, 1, LANES), lambda b: (b, 0, 0))],
        out_specs=[pl.BlockSpec((1, N_ROWQ * N_UNITS, t), lambda b: (b, 0, 0)),
                   pl.BlockSpec((1, t, LANES), lambda b: (b, 0, 0)),
                   pl.BlockSpec((1, N_UNITS, nc, LANES), lambda b: (b, 0, 0, 0)),
                   pl.BlockSpec((1, 1, LANES), lambda b: (b, 0, 0))],
        out_shape=[jax.ShapeDtypeStruct((bsz, N_ROWQ * N_UNITS, t), F32),
                   jax.ShapeDtypeStruct((bsz, t, LANES), F32),
                   jax.ShapeDtypeStruct((bsz, N_UNITS, nc, LANES), F32),
                   jax.ShapeDtypeStruct((bsz, 1, LANES), F32)],
        scratch_shapes=[tl() for _ in range(6)] + [cl() for _ in range(6)],
        compiler_params=_params(1),
        name="gates",
    )(gt, m0)


def _mlstm_kernel(q_ref, k_ref, v_ref, rows_ref, rcol_ref, sp_ref, c0_ref, h_ref, cf_ref,
                  vt_s, cin_s, st_s):
    t = q_ref.shape[1]
    n_pairs = t // PAIR
    row = lax.broadcasted_iota(jnp.int32, (PAIR, PAIR), 0)
    col = lax.broadcasted_iota(jnp.int32, (PAIR, PAIR), 1)
    same = (row >= CHUNK) == (col >= CHUNK)
    masks = (same & (col >= row), same & (col <= row))
    lane = lax.broadcasted_iota(jnp.int32, (1, PAIR), 1)
    halves = (lane < CHUNK, lane >= CHUNK)
    qrow = lax.broadcasted_iota(jnp.int32, (PAIR, HEAD_DIM), 0)
    nt_dims = (((1,), (1,)), ((), ()))
    heads = [slice(hd * HEAD_DIM, (hd + 1) * HEAD_DIM) for hd in range(HEADS)]

    def rowq(kind, u, p):
        return rows_ref[0, kind * N_UNITS + u, pl.ds(p, 1), :]

    for hd, hs in enumerate(heads):
        for blk in range(n_pairs):
            vt_s[hd, blk, 0:HEAD_DIM, :] = (
                v_ref[0, blk * PAIR:(blk + 1) * PAIR, hs].astype(F32).T.astype(BF16))
            vt_s[hd, blk, HEAD_DIM:, :] = jnp.ones((N_ROWS, PAIR), BF16)
    st_s[...] = c0_ref[0]

    def state_body(i, carry):
        for direction in range(2):
            p = i if direction == 0 else n_pairs - 1 - i
            r0 = pl.multiple_of(p * PAIR, PAIR)
            for hd, hs in enumerate(heads):
                u = direction * HEADS + hd
                vw = vt_s[hd, p].astype(F32) * rowq(ROW_WSTATE, u, p)
                lhs = jnp.concatenate([jnp.where(halves[0], vw, 0.0).astype(BF16),
                                       jnp.where(halves[1], vw, 0.0).astype(BF16)], axis=0)
                loc = jnp.dot(lhs, k_ref[0, pl.ds(r0, PAIR), hs], preferred_element_type=F32)
                st = st_s[u]
                for j in ((0, 1) if direction == 0 else (1, 0)):
                    cin_s[hd, p, direction * ST_ROWS:(direction + 1) * ST_ROWS,
                          j * HEAD_DIM:(j + 1) * HEAD_DIM] = st.astype(BF16)
                    st = (st * sp_ref[0, u, pl.ds(2 * p + j, 1), :]
                          + loc[j * ST_ROWS:(j + 1) * ST_ROWS])
                st_s[u] = st
        return carry

    lax.fori_loop(0, n_pairs, state_body, 0, unroll=2)
    cf_ref[0] = st_s[...]

    n_items = HEADS

    def out_body(p, carry):
        val = [dict() for _ in range(n_items)]
        r0 = pl.multiple_of(p * PAIR, PAIR)
        rcol = rcol_ref[0, pl.ds(r0, PAIR), :]

        def row_after(x, zero_bits):
            if zero_bits is None:
                return x
            return pltpu.bitcast(pltpu.bitcast(x, jnp.uint32) | zero_bits[0:1], x.dtype)

        def where(item):
            return p, r0, rcol, item

        def scores(item):
            p, r0, rcol, hd = where(item)
            hs = heads[hd]
            qp = q_ref[0, pl.ds(r0, PAIR), hs]
            val[item]["s_t"] = lax.dot_general(k_ref[0, pl.ds(r0, PAIR), hs], qp, nt_dims,
                                               preferred_element_type=F32)
            q_blk = jnp.concatenate([jnp.where(qrow < CHUNK, qp, 0),
                                     jnp.where(qrow >= CHUNK, qp, 0)], axis=1)
            inter = lax.dot_general(cin_s[hd, p], q_blk, nt_dims,
                                    preferred_element_type=F32)
            val[item]["inter"] = inter
            val[item]["scores_done"] = _zero_bits_after(inter)

        def weights(item):
            p, r0, rcol, hd = where(item)
            nxt = val[item + 1]["scores_done"] if item + 1 < n_items else None
            pts = []
            for direction in range(2):
                u = direction * HEADS + hd
                w = jnp.exp(rcol[:, u:u + 1] - row_after(rowq(ROW_M, u, p), nxt))
                pts.append((val[item]["s_t"] * jnp.where(masks[direction], w, 0.0)).astype(BF16))
            val[item]["pt"] = jnp.concatenate(pts, axis=1)

        def intra(item):
            p, r0, rcol, hd = where(item)
            res = jnp.dot(vt_s[hd, p], val[item]["pt"], preferred_element_type=F32)
            val[item]["intra"] = res
            val[item]["intra_done"] = _zero_bits_after(res)

        def combine(item):
            p, r0, rcol, hd = where(item)
            nxt = val[item + 1]["intra_done"] if item + 1 < n_items else None
            h_t = jnp.zeros((HEAD_DIM, PAIR), F32)
            for direction in range(2):
                u = direction * HEADS + hd
                tot = (val[item]["intra"][:, direction * PAIR:(direction + 1) * PAIR]
                       + row_after(rowq(ROW_WINTER, u, p), nxt)
                       * val[item]["inter"][direction * ST_ROWS:(direction + 1) * ST_ROWS])
                den = jnp.maximum(jnp.abs(tot[HEAD_DIM:HEAD_DIM + 1]), rowq(ROW_ENEGM, u, p))
                h_t = h_t + tot[0:HEAD_DIM] * (1.0 / den)
            h_ref[0, pl.ds(r0, PAIR), heads[hd]] = h_t.T.astype(BF16)

        for stage, item in ((scores, 0), (scores, 1), (scores, 2), (scores, 3), (weights, 0),
                            (weights, 1), (intra, 0), (weights, 2), (intra, 1), (weights, 3),
                            (intra, 2), (combine, 0), (intra, 3), (combine, 1), (combine, 2),
                            (combine, 3)):
            stage(item)
        return carry

    lax.fori_loop(0, n_pairs, out_body, 0)


def _mlstm(q, k, v, rows, rcol, sp, c0):
    bsz, t, _ = q.shape
    nc = t // CHUNK
    n_pairs = t // PAIR
    tok = pl.BlockSpec((1, t, MLSTM_DIM), lambda b: (b, 0, 0))
    st = pl.BlockSpec((1, N_UNITS, ST_ROWS, HEAD_DIM), lambda b: (b, 0, 0, 0))
    return pl.pallas_call(
        _mlstm_kernel,
        grid=(bsz,),
        in_specs=[tok, tok, tok,
                  pl.BlockSpec((1, N_ROWQ * N_UNITS, n_pairs, PAIR), lambda b: (b, 0, 0, 0)),
                  pl.BlockSpec((1, t, LANES), lambda b: (b, 0, 0)),
                  pl.BlockSpec((1, N_UNITS, nc, LANES), lambda b: (b, 0, 0, 0)),
                  st],
        out_specs=[tok, st],
        out_shape=[jax.ShapeDtypeStruct((bsz, t, MLSTM_DIM), BF16),
                   jax.ShapeDtypeStruct((bsz, N_UNITS, ST_ROWS, HEAD_DIM), F32)],
        scratch_shapes=[pltpu.VMEM((HEADS, n_pairs, ST_ROWS, PAIR), BF16),
                        pltpu.VMEM((HEADS, n_pairs, 2 * ST_ROWS, 2 * HEAD_DIM), BF16),
                        pltpu.VMEM((N_UNITS, ST_ROWS, HEAD_DIM), F32)],
        compiler_params=_params(1),
        name="mlstm",
    )(q, k, v, rows, rcol, sp, c0)


def _post_kernel(seg, x_ref, cu_ref, hm_ref, so_ref, mod_ref, hg_ref, ng_ref, wo_ref, wup_ref,
                 fw_ref, fb_ref, wdn_ref, o_ref, *scratch):
    tm = x_ref.shape[1]
    sub = min(tm, SUB)
    pos = lax.broadcasted_iota(jnp.int32, (sub, FF_BLOCK), 0) % seg
    first = pos == 0
    last = pos == seg - 1
    n_sub = tm // sub
    token = [None]

    def mixer_units(r):
        rs = slice(r * sub, (r + 1) * sub)
        x1_s, h2_s = scratch[3 * r + 1], scratch[3 * r + 2]
        y = [None]

        def outproj():
            hm = hm_ref[0, rs, :].astype(F32)
            so = so_ref[0, rs, :].astype(F32)
            parts = []
            for hd in range(HEADS):
                hs = slice(hd * HEAD_DIM, (hd + 1) * HEAD_DIM)
                hh = hm[:, hs]
                mu = jnp.mean(hh, axis=-1, keepdims=True)
                var = jnp.mean(jnp.square(hh - mu), axis=-1, keepdims=True)
                parts.append(so[:, hs] * ((hh - mu) * lax.rsqrt(var + EPS) * hg_ref[0, :, hs]))
            m = jnp.concatenate(parts, axis=-1).astype(BF16)
            y[0] = (jnp.dot(cu_ref[0, rs, :], wo_ref[0, 0:CONV_DIM, :], preferred_element_type=F32)
                    + jnp.dot(m, wo_ref[0, CONV_DIM:, :], preferred_element_type=F32))
            token[0] = _zero_bits_after(y[0])

        def residual():
            x1 = x_ref[0, rs, :] + mod_ref[0, 0, 2:3, :] * _rms(y[0], ng_ref[0, 1:2, :])
            x1_s[...] = x1
            h2_s[...] = (_rms(x1, ng_ref[0, 2:3, :]) * (1.0 + mod_ref[0, 0, 4:5, :])
                         + mod_ref[0, 0, 3:4, :]).astype(BF16)
            token[0] = _zero_bits_after(x1)

        return [outproj, residual]

    def ffn_units(r):
        rs = slice(r * sub, (r + 1) * sub)
        act_s, x1_s, h2_s = scratch[3 * r], scratch[3 * r + 1], scratch[3 * r + 2]

        def block(j):
            cs = slice(j * FF_BLOCK, (j + 1) * FF_BLOCK)
            gs = slice(D_FF + j * FF_BLOCK, D_FF + (j + 1) * FF_BLOCK)
            val = jnp.dot(h2_s[...], wup_ref[0, :, cs], preferred_element_type=F32)
            gate = jnp.dot(h2_s[...], wup_ref[0, :, gs], preferred_element_type=F32)
            gate = _ordered_after(gate, token[0])
            before = jnp.where(first, 0.0, pltpu.roll(gate, 1, 0))
            after = jnp.where(last, 0.0, pltpu.roll(gate, sub - 1, 0))
            conv = (fw_ref[0, 0:1, cs] * before + fw_ref[0, 1:2, cs] * gate
                    + fw_ref[0, 2:3, cs] * after + fb_ref[0, :, cs])
            act_s[:, cs] = (_silu(conv) * val).astype(BF16)

        def down():
            ffn = jnp.dot(act_s[...], wdn_ref[0], preferred_element_type=F32)
            o_ref[0, rs, :] = x1_s[...] + mod_ref[0, 0, 5:6, :] * _rms(ffn, ng_ref[0, 3:4, :])

        return [functools.partial(block, j) for j in range(D_FF // FF_BLOCK)] + [down]

    _interleave(mixer_units(0), [])
    for r in range(n_sub):
        token[0] = None
        _interleave(ffn_units(r), mixer_units(r + 1) if r + 1 < n_sub else [])


def _post(x, cu, hm, so, mods, l, row, p, seg, tm):
    bsz, t, d = x.shape
    sub = min(tm, SUB)
    tok = lambda n: pl.BlockSpec((1, tm, n), lambda b, i: (b, i, 0))
    acts = [x, cu, hm, so]
    big = [p["w_out"], p["w_up"]]
    return pl.pallas_call(
        functools.partial(_post_kernel, seg),
        grid=(bsz, t // tm),
        in_specs=[tok(a.shape[-1]) for a in acts]
                 + [_mod_spec(mods, l, row), _layer_spec(p["head_g"], l),
                    _layer_spec(p["norm_g"], l)]
                 + [_layer_spec(a, l, True) for a in big]
                 + [_layer_spec(p["fw"], l), _layer_spec(p["fb"], l),
                    _layer_spec(p["w_down"], l, True)],
        out_specs=tok(d),
        out_shape=jax.ShapeDtypeStruct((bsz, t, d), F32),
        scratch_shapes=[pltpu.VMEM((sub, D_FF), BF16), pltpu.VMEM((sub, d), F32),
                        pltpu.VMEM((sub, d), BF16)] * (tm // sub),
        compiler_params=_params(2),
        name="post",
    )(*acts, mods, p["head_g"], p["norm_g"], *big, p["fw"], p["fb"], p["w_down"])


def _mixer(xs, mods, l, row, p, seg, tm, m0, ct0):
    bsz, t, _ = xs.shape
    cu, q, k, v, so, gt = _inproj(xs, mods, l, row, p, seg, tm)
    rows, rcol, sp, mfin = _gates(gt, m0)
    rows = rows.reshape(bsz, N_ROWQ * N_UNITS, t // PAIR, PAIR)
    hm, ctf = _mlstm(q, k, v, rows, rcol, sp, ct0)
    return cu, hm, so, mfin, ctf


def kernel(x, c, ctx, c_ctx, w_ada, b_ada, norm_g, w_in, b_gates, conv_w, conv_b, conv_ln_g,
           conv_ln_b, mlstm_norm_g, w_out, w_up, ffn_conv_w, ffn_conv_b, w_down):
    bsz, t, d = x.shape
    t_ctx = ctx.shape[1]
    depth = w_ada.shape[0]
    rows = -(-(bsz + 1) // 8) * 8
    cond = jnp.zeros((rows, d), F32).at[:bsz].set(c).at[bsz].set(c_ctx)
    mods = _ada(cond, w_ada, b_ada).reshape(depth, rows, N_MOD, d)

    wg_raw = w_in[:, :, G_OFF:].reshape(depth, d, 2, 2, HEADS)
    bg_raw = b_gates.reshape(depth, 1, 2, 2, HEADS)
    pad = lambda a: jnp.pad(a.reshape(a.shape[0], a.shape[1], N_UNITS),
                            ((0, 0), (0, 0), (0, LANES - N_UNITS)))
    p = {
        "norm_g": norm_g,
        "w_inp": w_in.astype(BF16),
        "w_gate": jnp.concatenate([pad(wg_raw[:, :, :, 0]), pad(wg_raw[:, :, :, 1])],
                                  -1).astype(BF16),
        "bg": jnp.concatenate([pad(bg_raw[:, :, :, 0]), pad(bg_raw[:, :, :, 1])], -1),
        "conv_w": conv_w, "conv_b": conv_b[:, None], "ln_g": conv_ln_g[:, None],
        "ln_b": conv_ln_b[:, None], "head_g": mlstm_norm_g[:, None],
        "w_out": w_out.astype(BF16), "w_up": w_up.astype(BF16),
        "fw": ffn_conv_w, "fb": ffn_conv_b[:, None], "w_down": w_down.astype(BF16),
    }
    m_zero = jnp.zeros((bsz, 1, LANES), F32)
    ct_zero = jnp.zeros((bsz, N_UNITS, ST_ROWS, HEAD_DIM), F32)
    tm_in, tm_post = min(TM_INPROJ, t), min(TM_POST, t)
    cs = ctx
    for l in range(depth):
        cu, hm, so, m_c, ct_c = _mixer(cs, mods, l, bsz, p, t_ctx, t_ctx, m_zero, ct_zero)
        if l < depth - 1:
            cs = _post(cs, cu, hm, so, mods, l, bsz, p, t_ctx, t_ctx)
        cu, hm, so, _, _ = _mixer(x, mods, l, None, p, GRID_W, tm_in, m_c, ct_c)
        x = _post(x, cu, hm, so, mods, l, None, p, GRID_W, tm_post)
    return x
```
